```python
import numpy as np
import jax
import jax.numpy as jnp
from jax import lax

D_MODEL = 1024
BATCH = 4
SEQ = 8192
DEPTH = 2
DEC_BATCH = 128
DEC_SEQ = 8
PAST_LEN = 16384
PAGE_SIZE = 128

HD = 64
A_HEADS = 4
IDX_HEADS = 8
IDX_DIM = 64
DSA_TOPK = 256
B_HEADS = 8
Q_LORA = 256
KV_LORA = 128
D_NOPE = 64
D_ROPE = 32
D_V = 64
C_HEADS = 4
CMP_LEN = 32
CMP_STRIDE = 16
SLC_BLOCK = 64
N_SLC = 16
WINDOW = 512
N_GATES = 3
D_MIX = A_HEADS * HD + B_HEADS * D_V + C_HEADS * HD
D_FF = 2816
ROPE_THETA = 10000.0
NORM_EPS = 1e-6
QBLOCK = 128
FORCE_SCORE = 1e9

SPLIT_SIZES = (A_HEADS * HD, HD, HD, IDX_HEADS * IDX_DIM, IDX_HEADS, IDX_DIM,
               Q_LORA, KV_LORA, D_ROPE,
               C_HEADS * HD, HD, HD, HD, HD, HD, HD, C_HEADS * N_GATES)
IN_COLS = sum(SPLIT_SIZES)
SPLIT_OFFSETS = tuple(int(v) for v in np.cumsum(SPLIT_SIZES)[:-1])

kernel_name = 'hybrid_dsa_mla_nsa_macaron_step'


def _rmsnorm(x, g):
    xf = x.astype(jnp.float32)
    r = lax.rsqrt(jnp.mean(xf * xf, axis=-1, keepdims=True) + NORM_EPS)
    return (xf * r * g.astype(jnp.float32)).astype(x.dtype)


def _rope(x, pos):
    half = x.shape[-1] // 2
    inv = ROPE_THETA ** (-jnp.arange(half, dtype=jnp.float32) / half)
    ang = pos.astype(jnp.float32)[:, None] * inv[None, :]
    cos = jnp.cos(ang)[:, None, :]
    sin = jnp.sin(ang)[:, None, :]
    x1 = x[..., :half].astype(jnp.float32)
    x2 = x[..., half:].astype(jnp.float32)
    return jnp.concatenate([x1 * cos - x2 * sin, x2 * cos + x1 * sin], axis=-1).astype(x.dtype)


def _masked_softmax(s, mask):
    s = jnp.where(mask, s.astype(jnp.float32), -jnp.inf)
    m = jnp.max(s, axis=-1, keepdims=True)
    m = jnp.where(jnp.isfinite(m), m, 0.0)
    e = jnp.exp(s - m)
    return e / jnp.maximum(jnp.sum(e, axis=-1, keepdims=True), jnp.finfo(jnp.float32).tiny)


def _swiglu(h, w13, w2):
    g, u = jnp.split(h @ w13, 2, axis=-1)
    return (jax.nn.silu(g) * u) @ w2


def _sweep(fn, q_pos, *qs):
    t = q_pos.shape[0]
    if t <= QBLOCK or t % QBLOCK:
        return fn(0, q_pos, *qs)
    nb = t // QBLOCK

    def to_blocks(a):
        return jnp.moveaxis(a.reshape((a.shape[0], nb, QBLOCK) + a.shape[2:]), 1, 0)

    starts = jnp.arange(nb, dtype=jnp.int32) * QBLOCK
    out = lax.map(lambda xs: fn(xs[0], xs[1], *xs[2]),
                  (starts, q_pos.reshape(nb, QBLOCK), tuple(to_blocks(a) for a in qs)))
    out = jnp.moveaxis(out, 0, 1)
    return out.reshape((out.shape[0], t) + out.shape[3:])


def _take_rows(rows, idx):
    b = idx.shape[0]
    flat = jnp.clip(idx.reshape(b, -1), 0, rows.shape[1] - 1)
    out = jnp.take_along_axis(rows, flat[..., None], axis=1)
    return out.reshape(idx.shape + (rows.shape[-1],))


def _compress(rows, w_pos, w_proj):
    b, l, d = rows.shape
    c = CMP_LEN // CMP_STRIDE
    n_chunk = l // CMP_STRIDE
    n_cmp = n_chunk - c + 1
    r = rows[:, :n_chunk * CMP_STRIDE].reshape(b, n_chunk, CMP_STRIDE, d)
    blk = None
    for m in range(c):
        part = jnp.einsum('bcjd,jd->bcd', r, w_pos[m * CMP_STRIDE:(m + 1) * CMP_STRIDE])[:, m:m + n_cmp]
        blk = part if blk is None else blk + part
    return blk @ w_proj


def _slc_importance(pc, n_slc):
    r = SLC_BLOCK // CMP_STRIDE
    c = CMP_LEN // CMP_STRIDE
    n_cmp = pc.shape[-1]
    pp = jnp.pad(pc, ((0, 0), (0, 0), (c - 1, r * n_slc - n_cmp)))
    imp = None
    for m in range(r + c - 1):
        part = pp[..., m:m + r * (n_slc - 1) + 1:r]
        imp = part if imp is None else imp + part
    return imp


def _prompt_access():
    def full(kind, new):
        return new

    def fetch(kind, new, idx):
        return _take_rows(new, idx)

    def window(new, qstart, tb):
        padded = jnp.pad(new, ((0, 0), (WINDOW, 0), (0, 0)))
        rows = lax.dynamic_slice_in_dim(padded, qstart, WINDOW + tb, axis=1)
        pos = qstart - WINDOW + jnp.arange(WINDOW + tb, dtype=jnp.int32)
        return rows, pos

    def win_state(new):
        return new[:, new.shape[1] - min(WINDOW, new.shape[1]):]

    return full, fetch, window, win_state


def _sample_access(l, pools, state_win, page_table):
    n_pages = page_table.shape[1]
    past_len = n_pages * PAGE_SIZE

    def full(kind, new):
        pool = pools[kind]
        past = pool[l, page_table].reshape(new.shape[0], past_len, pool.shape[-1])
        return jnp.concatenate([past, new], axis=1)

    def fetch(kind, new, idx):
        pool = pools[kind]
        b = idx.shape[0]
        flat = idx.reshape(b, -1)
        pidx = jnp.clip(flat, 0, past_len - 1)
        phys = jnp.take_along_axis(page_table, pidx // PAGE_SIZE, axis=1)
        past = pool[l, phys, pidx % PAGE_SIZE]
        nidx = jnp.clip(flat - past_len, 0, new.shape[1] - 1)
        cur = jnp.take_along_axis(new, nidx[..., None], axis=1)
        out = jnp.where((flat < past_len)[..., None], past, cur)
        return out.reshape(idx.shape + (pool.shape[-1],))

    def window(new, qstart, tb):
        rows = jnp.concatenate([state_win[l], new], axis=1)
        pos = past_len - state_win.shape[2] + jnp.arange(rows.shape[1], dtype=jnp.int32)
        return rows, pos

    def win_state(new):
        rows = jnp.concatenate([state_win[l], new], axis=1)
        return rows[:, rows.shape[1] - min(WINDOW, rows.shape[1]):]

    return full, fetch, window, win_state


def _mixers(h, l, past_len, access, p):
    full, fetch, window, win_state = access
    b, t, _ = h.shape
    L = past_len + t
    q_pos = past_len + jnp.arange(t, dtype=jnp.int32)
    key_pos = jnp.arange(L, dtype=jnp.int32)
    (a_q, a_k, a_v, a_iq, a_iw, a_ik, b_cq, b_ckv, b_kr,
     c_q, c_kc, c_vc, c_ks, c_vs, c_kw, c_vw, c_g) = jnp.split(h @ p['w_in'][l], SPLIT_OFFSETS, axis=-1)

    a_q = _rope(a_q.reshape(b, t, A_HEADS, HD), q_pos)
    a_kv_new = jnp.concatenate([_rope(a_k[:, :, None], q_pos)[:, :, 0], a_v], axis=-1)
    a_iq = _rope(a_iq.reshape(b, t, IDX_HEADS, IDX_DIM), q_pos)
    a_ik_new = _rope(a_ik[:, :, None], q_pos)[:, :, 0]
    a_iw = a_iw * IDX_HEADS ** -0.5
    k_idx_all = full('a_kidx', a_ik_new)
    n_keep = min(DSA_TOPK, L // 4)

    def a_block(qstart, qpos, qa, qi, wi):
        rel = jax.nn.relu(jnp.einsum('bthd,bsd->bths', qi, k_idx_all).astype(jnp.float32))
        score = jnp.einsum('bths,bth->bts', rel, wi.astype(jnp.float32)) * IDX_DIM ** -0.5
        score = jnp.where((key_pos[None, :] <= qpos[:, None])[None], score, -jnp.inf)
        _, sel = lax.top_k(score, n_keep)
        kv = fetch('a_kv', a_kv_new, sel)
        s = jnp.einsum('bthd,btkd->bhtk', qa, kv[..., :HD]) * HD ** -0.5
        pr = _masked_softmax(s, (sel <= qpos[None, :, None])[:, None])
        return jnp.einsum('bhtk,btkd->bthd', pr.astype(kv.dtype), kv[..., HD:])

    o_a = _sweep(a_block, q_pos, a_q, a_iq, a_iw)

    q_b = (_rmsnorm(b_cq, p['b_q_norm'][l]) @ p['b_w_uq'][l]).reshape(b, t, B_HEADS, D_NOPE + D_ROPE)
    q_lat = jnp.einsum('bthd,chd->bthc', q_b[..., :D_NOPE], p['b_w_uk'][l])
    q_rope = _rope(q_b[..., D_NOPE:], q_pos)
    b_ckv_new = jnp.concatenate([_rmsnorm(b_ckv, p['b_kv_norm'][l]),
                                 _rope(b_kr[:, :, None], q_pos)[:, :, 0]], axis=-1)
    ckv_all = full('b_ckv', b_ckv_new)
    lat_all = ckv_all[..., :KV_LORA]
    kr_all = ckv_all[..., KV_LORA:]

    def b_block(qstart, qpos, ql, qr):
        s = (jnp.einsum('bthc,bsc->bhts', ql, lat_all)
             + jnp.einsum('bthr,bsr->bhts', qr, kr_all)) * (D_NOPE + D_ROPE) ** -0.5
        pr = _masked_softmax(s, (key_pos[None, :] <= qpos[:, None])[None, None])
        return jnp.einsum('bhts,bsc->bthc', pr.astype(lat_all.dtype), lat_all)

    o_b = jnp.einsum('bthc,chd->bthd', _sweep(b_block, q_pos, q_lat, q_rope), p['b_w_uv'][l])

    c_q = c_q.reshape(b, t, C_HEADS, HD)
    c_q_rot = _rope(c_q, q_pos)
    gates = jax.nn.sigmoid(c_g.astype(jnp.float32)).astype(h.dtype).reshape(b, t, C_HEADS, N_GATES)
    c_cmp_new = jnp.concatenate([c_kc, c_vc], axis=-1)
    cmp_all = full('c_cmp', c_cmp_new)
    k_cmp = _compress(cmp_all[..., :HD], p['c_cmp_pos_k'][l], p['c_cmp_proj_k'][l])
    v_cmp = _compress(cmp_all[..., HD:], p['c_cmp_pos_v'][l], p['c_cmp_proj_v'][l])
    cmp_end = jnp.arange(k_cmp.shape[1], dtype=jnp.int32) * CMP_STRIDE + CMP_LEN - 1
    c_slc_new = jnp.concatenate([_rope(c_ks[:, :, None], q_pos)[:, :, 0], c_vs], axis=-1)
    c_win_new = jnp.concatenate([_rope(c_kw[:, :, None], q_pos)[:, :, 0], c_vw], axis=-1)
    n_slc = -(-L // SLC_BLOCK)
    n_top = min(N_SLC, n_slc)
    blk_idx = jnp.arange(n_slc, dtype=jnp.int32)
    scale = HD ** -0.5

    def c_block(qstart, qpos, q, qr, g):
        tb = qpos.shape[0]
        s = jnp.einsum('bthd,bnd->bhtn', q, k_cmp) * scale
        p_cmp = _masked_softmax(s, (cmp_end[None, :] <= qpos[:, None])[None, None])
        o_cmp = jnp.einsum('bhtn,bnd->bthd', p_cmp.astype(v_cmp.dtype), v_cmp)
        imp = _slc_importance(jnp.sum(p_cmp, axis=1), n_slc)
        cur = (qpos // SLC_BLOCK)[:, None]
        forced = (blk_idx[None, :] == 0) | (blk_idx[None, :] == cur) | (blk_idx[None, :] == cur - 1)
        adm = blk_idx[None, :] * SLC_BLOCK <= qpos[:, None]
        score = jnp.where(forced[None], FORCE_SCORE, jnp.where(adm[None], imp, -jnp.inf))
        _, sel = lax.top_k(score, n_top)
        tok = (sel[..., None] * SLC_BLOCK + jnp.arange(SLC_BLOCK, dtype=jnp.int32)).reshape(b, tb, n_top * SLC_BLOCK)
        kv = fetch('c_slc', c_slc_new, tok)
        s = jnp.einsum('bthd,btkd->bhtk', qr, kv[..., :HD]) * scale
        pr = _masked_softmax(s, (tok <= qpos[None, :, None])[:, None])
        o_slc = jnp.einsum('bhtk,btkd->bthd', pr.astype(kv.dtype), kv[..., HD:])
        wrows, wpos = window(c_win_new, qstart, tb)
        dist = qpos[:, None] - wpos[None, :]
        wmask = (dist >= 0) & (dist < WINDOW) & (wpos[None, :] >= 0)
        s = jnp.einsum('bthd,bsd->bhts', qr, wrows[..., :HD]) * scale
        pr = _masked_softmax(s, wmask[None, None])
        o_win = jnp.einsum('bhts,bsd->bthd', pr.astype(wrows.dtype), wrows[..., HD:])
        return g[..., 0:1] * o_cmp + g[..., 1:2] * o_slc + g[..., 2:3] * o_win

    o_c = _sweep(c_block, q_pos, c_q, c_q_rot, gates)

    o = jnp.concatenate([o_a.reshape(b, t, A_HEADS * HD),
                         o_b.reshape(b, t, B_HEADS * D_V),
                         o_c.reshape(b, t, C_HEADS * HD)], axis=-1)
    new_rows = (a_kv_new, a_ik_new, b_ckv_new, c_cmp_new, c_slc_new, win_state(c_win_new))
    return o @ p['w_out'][l], new_rows


def _layer(x, l, past_len, access, p):
    x = x + 0.5 * _swiglu(_rmsnorm(x, p['norm_ffn_pre'][l]), p['ffn_pre_w13'][l], p['ffn_pre_w2'][l])
    mix, rows = _mixers(_rmsnorm(x, p['norm_mix'][l]), l, past_len, access, p)
    x = x + mix
    x = x + 0.5 * _swiglu(_rmsnorm(x, p['norm_ffn_post'][l]), p['ffn_post_w13'][l], p['ffn_post_w2'][l])
    return x, rows


def _trunk(x, past_len, access_for_layer, p):
    per_layer = []
    for l in range(DEPTH):
        x, rows = _layer(x, l, past_len, access_for_layer(l), p)
        per_layer.append(rows)
    y = _rmsnorm(x, p['norm_final'])
    stacked = tuple(jnp.stack(r, axis=0) for r in zip(*per_layer))
    return y, stacked


def setup_inputs(seed: int = 0) -> dict:
    key = jax.random.key(seed)
    ks = jax.random.split(key, 32)
    f32 = jnp.float32
    n_pages = PAST_LEN // PAGE_SIZE
    n_used = DEC_BATCH * n_pages
    n_pool = n_used + n_used // 4
    win_buf = min(WINDOW, PAST_LEN)

    def nrm(k, shape, scale=1.0):
        return jax.random.normal(k, shape, f32) * scale

    def gain(k, shape):
        return 1.0 + 0.05 * jax.random.normal(k, shape, f32)

    page_table = jax.random.permutation(ks[8], n_pool)[:n_used].reshape(DEC_BATCH, n_pages).astype(jnp.int32)
    return {
        'x_prompt': nrm(ks[0], (BATCH, SEQ, D_MODEL)),
        'x_sample': nrm(ks[1], (DEC_BATCH, DEC_SEQ, D_MODEL)),
        'cache_a_kv': nrm(ks[2], (DEPTH, n_pool, PAGE_SIZE, 2 * HD)),
        'cache_a_kidx': nrm(ks[3], (DEPTH, n_pool, PAGE_SIZE, IDX_DIM)),
        'cache_b_ckv': nrm(ks[4], (DEPTH, n_pool, PAGE_SIZE, KV_LORA + D_ROPE)),
        'cache_c_cmp': nrm(ks[5], (DEPTH, n_pool, PAGE_SIZE, 2 * HD)),
        'cache_c_slc': nrm(ks[6], (DEPTH, n_pool, PAGE_SIZE, 2 * HD)),
        'state_c_win': nrm(ks[7], (DEPTH, DEC_BATCH, win_buf, 2 * HD)),
        'page_table': page_table,
        'w_in': nrm(ks[9], (DEPTH, D_MODEL, IN_COLS), D_MODEL ** -0.5),
        'w_out': nrm(ks[10], (DEPTH, D_MIX, D_MODEL), D_MIX ** -0.5),
        'b_w_uq': nrm(ks[11], (DEPTH, Q_LORA, B_HEADS * (D_NOPE + D_ROPE)), Q_LORA ** -0.5),
        'b_w_uk': nrm(ks[12], (DEPTH, KV_LORA, B_HEADS, D_NOPE), KV_LORA ** -0.5),
        'b_w_uv': nrm(ks[13], (DEPTH, KV_LORA, B_HEADS, D_V), KV_LORA ** -0.5),
        'b_q_norm': gain(ks[14], (DEPTH, Q_LORA)),
        'b_kv_norm': gain(ks[15], (DEPTH, KV_LORA)),
        'c_cmp_pos_k': (1.0 + 0.1 * nrm(ks[16], (DEPTH, CMP_LEN, HD))) * CMP_LEN ** -0.5,
        'c_cmp_pos_v': (1.0 + 0.1 * nrm(ks[17], (DEPTH, CMP_LEN, HD))) * CMP_LEN ** -0.5,
        'c_cmp_proj_k': nrm(ks[18], (DEPTH, HD, HD), HD ** -0.5),
        'c_cmp_proj_v': nrm(ks[19], (DEPTH, HD, HD), HD ** -0.5),
        'ffn_pre_w13': nrm(ks[20], (DEPTH, D_MODEL, 2 * D_FF), D_MODEL ** -0.5),
        'ffn_pre_w2': nrm(ks[21], (DEPTH, D_FF, D_MODEL), D_FF ** -0.5),
        'ffn_post_w13': nrm(ks[22], (DEPTH, D_MODEL, 2 * D_FF), D_MODEL ** -0.5),
        'ffn_post_w2': nrm(ks[23], (DEPTH, D_FF, D_MODEL), D_FF ** -0.5),
        'norm_ffn_pre': gain(ks[24], (DEPTH, D_MODEL)),
        'norm_mix': gain(ks[25], (DEPTH, D_MODEL)),
        'norm_ffn_post': gain(ks[26], (DEPTH, D_MODEL)),
        'norm_final': gain(ks[27], (D_MODEL,)),
    }


def reference(x_prompt, x_sample, cache_a_kv, cache_a_kidx, cache_b_ckv, cache_c_cmp, cache_c_slc,
              state_c_win, page_table, w_in, w_out, b_w_uq, b_w_uk, b_w_uv, b_q_norm, b_kv_norm,
              c_cmp_pos_k, c_cmp_pos_v, c_cmp_proj_k, c_cmp_proj_v,
              ffn_pre_w13, ffn_pre_w2, ffn_post_w13, ffn_post_w2,
              norm_ffn_pre, norm_mix, norm_ffn_post, norm_final):
    p = {'w_in': w_in, 'w_out': w_out, 'b_w_uq': b_w_uq, 'b_w_uk': b_w_uk, 'b_w_uv': b_w_uv,
         'b_q_norm': b_q_norm, 'b_kv_norm': b_kv_norm,
         'c_cmp_pos_k': c_cmp_pos_k, 'c_cmp_pos_v': c_cmp_pos_v,
         'c_cmp_proj_k': c_cmp_proj_k, 'c_cmp_proj_v': c_cmp_proj_v,
         'ffn_pre_w13': ffn_pre_w13, 'ffn_pre_w2': ffn_pre_w2,
         'ffn_post_w13': ffn_post_w13, 'ffn_post_w2': ffn_post_w2,
         'norm_ffn_pre': norm_ffn_pre, 'norm_mix': norm_mix, 'norm_ffn_post': norm_ffn_post,
         'norm_final': norm_final}
    pools = {'a_kv': cache_a_kv, 'a_kidx': cache_a_kidx, 'b_ckv': cache_b_ckv,
             'c_cmp': cache_c_cmp, 'c_slc': cache_c_slc}
    past_len = page_table.shape[1] * PAGE_SIZE
    y_prompt, (a_kv_p, a_kidx_p, b_ckv_p, c_cmp_p, c_slc_p, c_win_p) = _trunk(
        x_prompt, 0, lambda l: _prompt_access(), p)
    y_sample, (a_kv_s, a_kidx_s, b_ckv_s, c_cmp_s, c_slc_s, c_win_s) = _trunk(
        x_sample, past_len, lambda l: _sample_access(l, pools, state_c_win, page_table), p)
    return (y_prompt, y_sample, a_kv_p, a_kv_s, a_kidx_p, a_kidx_s, b_ckv_p, b_ckv_s,
            c_cmp_p, c_cmp_s, c_slc_p, c_slc_s, c_win_p, c_win_s)
```

```python
import functools

import numpy as np
import jax
import jax.numpy as jnp
from jax import lax
from jax.experimental import pallas as pl
from jax.experimental.pallas import tpu as pltpu

HD = 64
A_HEADS = 4
IDX_HEADS = 8
IDX_DIM = 64
DSA_TOPK = 256
B_HEADS = 8
Q_LORA = 256
KV_LORA = 128
D_NOPE = 64
D_ROPE = 32
D_V = 64
C_HEADS = 4
CMP_LEN = 32
CMP_STRIDE = 16
SLC_BLOCK = 64
N_SLC = 16
WINDOW = 512
N_GATES = 3
ROPE_THETA = 10000.0
NORM_EPS = 1e-6
QBLOCK = 128
FORCE_SCORE = 1e9
PAGE = 128

LANES = 128
MXU_DTYPE = jnp.bfloat16
F32 = jnp.float32
I32 = jnp.int32
NEG = -1e30
INT_MIN = -2147483648
VMEM_LIMIT = 56 * 1024 * 1024
PAGES_PER_STEP = 16
KCHUNK = 512

MISC_W = IDX_DIM
MISC_G = IDX_DIM + IDX_HEADS


def _dot(a, b):
    return jnp.dot(a, b, preferred_element_type=F32)


def _dot_nt(a, b):
    return lax.dot_general(a, b, (((1,), (1,)), ((), ())), preferred_element_type=F32)


def _mx(x):
    return x.astype(MXU_DTYPE)


def _rms(x, g):
    r = lax.rsqrt(jnp.mean(x * x, axis=-1, keepdims=True) + NORM_EPS)
    return x * r * g


def _cparams(n_grid):
    return pltpu.CompilerParams(dimension_semantics=("arbitrary",) * n_grid,
                                vmem_limit_bytes=VMEM_LIMIT)


def _lane(shape):
    return lax.broadcasted_iota(I32, shape, len(shape) - 1)


def _row(shape):
    return lax.broadcasted_iota(I32, shape, len(shape) - 2)


def _order_key(score):
    score = jnp.where(score == 0.0, 0.0, score)
    bits = lax.bitcast_convert_type(score, I32)
    return bits ^ ((bits >> 31) & 0x7FFFFFFF)


def _kth_largest(count_ge, k, shape):
    def body(it, t):
        cand = t ^ jnp.left_shift(jnp.int32(1), 31 - it)
        return jnp.where(count_ge(cand) >= k, cand, t)
    return lax.fori_loop(0, 32, body, jnp.full(shape, INT_MIN, I32))


def _tie_cut(count_tie_below, need, nbits, shape):
    def body(it, cut):
        cand = cut | jnp.left_shift(jnp.int32(1), nbits - 1 - it)
        return jnp.where(count_tie_below(cand) <= need, cand, cut)
    return lax.fori_loop(0, nbits, body, jnp.zeros(shape, I32))


def _topk_mask(key, idx, k, nbits):
    rows = (key.shape[0], 1)

    def count(pred):
        return jnp.sum(jnp.where(pred, 1, 0), axis=-1, keepdims=True)

    thr = jnp.maximum(_kth_largest(lambda cand: count(key >= cand), k, rows), INT_MIN + 1)
    above = key > thr
    tied = key == thr
    need = k - count(above)
    cut = _tie_cut(lambda cand: count(tied & (idx < cand)), need, nbits, rows)
    return above | (tied & (idx < cut))


def _softmax_update(s, mask, m_ref, l_ref, acc_ref, v):
    m_old = m_ref[...]
    m_new = jnp.maximum(m_old, jnp.max(s, axis=-1, keepdims=True))
    p = jnp.exp(s - m_new)
    if mask is not None:
        p = jnp.where(mask, p, 0.0)
    alpha = jnp.exp(m_old - m_new)
    l_ref[...] = alpha * l_ref[...] + jnp.sum(p, axis=-1, keepdims=True)
    acc_ref[...] = alpha * acc_ref[...] + _dot(_mx(p), v)
    m_ref[...] = m_new


def _tile_rows(x, n):
    return jnp.concatenate([x] * n, axis=0)


def _ffn_kernel(n_mix, has_final, *refs):
    x_ref = refs[0]
    mix = refs[1:1 + 2 * n_mix]
    g_ref, w1_ref, w3_ref, w2_ref = refs[1 + 2 * n_mix:5 + 2 * n_mix]
    pos = 5 + 2 * n_mix
    gf_ref = refs[pos] if has_final else None
    pos += 1 if has_final else 0
    out_ref, xs, hs, acc = refs[pos:pos + 4]
    f = pl.program_id(1)

    @pl.when(f == 0)
    def _():
        x = x_ref[...]
        for i in range(n_mix):
            x = x + _dot(_mx(mix[2 * i][...]), mix[2 * i + 1][...])
        xs[...] = x
        hs[...] = _mx(_rms(x, g_ref[...]))
        acc[...] = jnp.zeros_like(acc)

    h = hs[...]
    gate = _dot(h, w1_ref[...])
    up = _dot(h, w3_ref[...])
    acc[...] += _dot(_mx(jax.nn.silu(gate) * up), w2_ref[...])

    @pl.when(f == pl.num_programs(1) - 1)
    def _():
        y = xs[...] + 0.5 * acc[...]
        if has_final:
            y = _rms(y, gf_ref[...])
        out_ref[...] = y


def _ffn(x, g, w13, w2, mix=(), final_g=None):
    n, d = x.shape
    d_ff = w2.shape[0]
    tm = min(1024, n)
    tf = 256
    nf = d_ff // tf
    assert n % tm == 0 and d_ff % tf == 0
    in_specs = [pl.BlockSpec((tm, d), lambda i, f: (i, 0))]
    args = [x]
    for o, w in mix:
        in_specs += [pl.BlockSpec((tm, o.shape[1]), lambda i, f: (i, 0)),
                     pl.BlockSpec(w.shape, lambda i, f: (0, 0))]
        args += [o, w]
    in_specs += [pl.BlockSpec((1, d), lambda i, f: (0, 0)),
                 pl.BlockSpec((d, tf), lambda i, f: (0, f)),
                 pl.BlockSpec((d, tf), lambda i, f: (0, f + nf)),
                 pl.BlockSpec((tf, d), lambda i, f: (f, 0))]
    args += [g.reshape(1, d), w13, w13, w2]
    if final_g is not None:
        in_specs.append(pl.BlockSpec((1, d), lambda i, f: (0, 0)))
        args.append(final_g.reshape(1, d))
    return pl.pallas_call(
        functools.partial(_ffn_kernel, len(mix), final_g is not None),
        out_shape=jax.ShapeDtypeStruct((n, d), F32),
        grid=(n // tm, nf),
        in_specs=in_specs,
        out_specs=pl.BlockSpec((tm, d), lambda i, f: (i, 0)),
        scratch_shapes=[pltpu.VMEM((tm, d), F32), pltpu.VMEM((tm, d), MXU_DTYPE),
                        pltpu.VMEM((tm, d), F32)],
        compiler_params=_cparams(2),
        name="ffn",
    )(*args)


C_AQ, C_AKV, C_AIQ, C_MISC, C_BCQ, C_BCKV, C_BKR, C_CQ, C_CCMP, C_CSLC, C_CWIN, C_END = (
    0, 256, 384, 896, 1024, 1280, 1408, 1536, 1792, 1920, 2048, 2176)
T_HH, T_HV, T_R1 = 0, 2, 4


def _rope(x, tab_ref, kind, half):
    cos = tab_ref[:, kind * LANES:(kind + 1) * LANES]
    ssin = tab_ref[:, (kind + 1) * LANES:(kind + 2) * LANES]
    first = (_lane(x.shape) % (2 * half)) < half
    rot = jnp.where(first, pltpu.roll(x, LANES - half, 1), pltpu.roll(x, half, 1))
    return x * cos + rot * ssin


def _split_heads(r):
    low = _lane(r.shape) < HD
    return jnp.where(low, r, 0.0), jnp.where(low, pltpu.roll(r, HD, 1), 0.0)


def _inproj_kernel(x_ref, g_ref, w_ref, tab_ref, qn_ref, wuq_ref, wuk_ref, kvn_ref,
                   aq_ref, akv_ref, akvb_ref, aiq_ref, misc_ref, aikb_ref, qcat_ref,
                   bckv_ref, kb_ref, cq_ref, cqr_ref, ccmp_ref, cslc_ref, cslcb_ref,
                   cwin_ref, cwinb_ref):
    h = _mx(_rms(x_ref[...], g_ref[...]))

    def proj(c0, c1):
        return _dot(h, w_ref[:, c0:c1])

    qscale = HD ** -0.5
    for i in range(A_HEADS // 2):
        r = _rope(proj(C_AQ + LANES * i, C_AQ + LANES * (i + 1)), tab_ref, T_HH, HD // 2) * qscale
        h0, h1 = _split_heads(r)
        aq_ref[2 * i] = h0.astype(aq_ref.dtype)
        aq_ref[2 * i + 1] = h1.astype(aq_ref.dtype)

    akv = _rope(proj(C_AKV, C_AIQ), tab_ref, T_HV, HD // 2)
    akv_ref[...] = akv
    akvb_ref[...] = _mx(akv)

    iscale = IDX_DIM ** -0.5
    for i in range(IDX_HEADS // 2):
        r = _rope(proj(C_AIQ + LANES * i, C_AIQ + LANES * (i + 1)), tab_ref, T_HH, IDX_DIM // 2) * iscale
        h0, h1 = _split_heads(r)
        aiq_ref[2 * i] = h0.astype(aiq_ref.dtype)
        aiq_ref[2 * i + 1] = h1.astype(aiq_ref.dtype)

    misc = _rope(proj(C_MISC, C_BCQ), tab_ref, T_HV, IDX_DIM // 2)
    lane = _lane(misc.shape)
    misc = jnp.where((lane >= MISC_W) & (lane < MISC_G), misc * IDX_HEADS ** -0.5, misc)
    misc = jnp.where((lane >= MISC_G) & (lane < MISC_G + C_HEADS * N_GATES),
                     jax.nn.sigmoid(misc), misc)
    misc_ref[...] = misc
    aikb_ref[...] = _mx(jnp.where(lane < IDX_DIM, misc, 0.0))

    bscale = (D_NOPE + D_ROPE) ** -0.5
    cqn = _mx(_rms(proj(C_BCQ, C_BCKV), qn_ref[...]))
    for p in range(B_HEADS // 2):
        nope = _mx(_dot(cqn, wuq_ref[:, LANES * p:LANES * (p + 1)]))
        lat = _dot(nope, wuk_ref[p]) * bscale
        qcat_ref[2 * p, :, 0:LANES] = lat[:, :LANES].astype(qcat_ref.dtype)
        qcat_ref[2 * p + 1, :, 0:LANES] = lat[:, LANES:].astype(qcat_ref.dtype)
    for hh in range(B_HEADS):
        c0 = B_HEADS * D_NOPE + LANES * hh
        qr = _rope(_dot(cqn, wuq_ref[:, c0:c0 + LANES]), tab_ref, T_R1, D_ROPE // 2) * bscale
        qcat_ref[hh, :, LANES:2 * LANES] = qr.astype(qcat_ref.dtype)

    lat = _rms(proj(C_BCKV, C_BKR), kvn_ref[...])
    kr = _rope(proj(C_BKR, C_CQ), tab_ref, T_R1, D_ROPE // 2)
    bckv_ref[:, 0:LANES] = lat
    bckv_ref[:, LANES:2 * LANES] = kr
    kb_ref[:, 0:LANES] = _mx(lat)
    kb_ref[:, LANES:2 * LANES] = _mx(kr)

    for i in range(C_HEADS // 2):
        z = proj(C_CQ + LANES * i, C_CQ + LANES * (i + 1)) * qscale
        h0, h1 = _split_heads(z)
        cq_ref[2 * i] = h0.astype(cq_ref.dtype)
        cq_ref[2 * i + 1] = h1.astype(cq_ref.dtype)
        h0, h1 = _split_heads(_rope(z, tab_ref, T_HH, HD // 2))
        cqr_ref[2 * i] = h0.astype(cqr_ref.dtype)
        cqr_ref[2 * i + 1] = h1.astype(cqr_ref.dtype)

    ccmp_ref[...] = proj(C_CCMP, C_CSLC)
    cslc = _rope(proj(C_CSLC, C_CWIN), tab_ref, T_HV, HD // 2)
    cslc_ref[...] = cslc
    cslcb_ref[...] = _mx(cslc)
    cwin = _rope(proj(C_CWIN, C_END), tab_ref, T_HV, HD // 2)
    cwin_ref[...] = cwin
    cwinb_ref[...] = _mx(cwin)


def _inproj(x, g, w, tab, qn, wuq, wuk, kvn, qdt):
    n, d = x.shape
    tm = min(1024, n)
    nt = tab.shape[0] // tm
    assert n % tm == 0 and tab.shape[0] % tm == 0
    row = lambda i: (i, 0)
    hrow = lambda i: (0, i, 0)
    const2 = lambda i: (0, 0)
    const3 = lambda i: (0, 0, 0)
    sds = jax.ShapeDtypeStruct
    out_shape = (
        sds((A_HEADS, n, LANES), qdt), sds((n, LANES), F32), sds((n, LANES), MXU_DTYPE),
        sds((IDX_HEADS, n, LANES), qdt), sds((n, LANES), F32), sds((n, LANES), MXU_DTYPE),
        sds((B_HEADS, n, 2 * LANES), qdt), sds((n, 2 * LANES), F32), sds((n, 2 * LANES), MXU_DTYPE),
        sds((C_HEADS, n, LANES), qdt), sds((C_HEADS, n, LANES), qdt),
        sds((n, LANES), F32), sds((n, LANES), F32), sds((n, LANES), MXU_DTYPE),
        sds((n, LANES), F32), sds((n, LANES), MXU_DTYPE))
    out_specs = (
        pl.BlockSpec((A_HEADS, tm, LANES), hrow), pl.BlockSpec((tm, LANES), row), pl.BlockSpec((tm, LANES), row),
        pl.BlockSpec((IDX_HEADS, tm, LANES), hrow), pl.BlockSpec((tm, LANES), row), pl.BlockSpec((tm, LANES), row),
        pl.BlockSpec((B_HEADS, tm, 2 * LANES), hrow), pl.BlockSpec((tm, 2 * LANES), row),
        pl.BlockSpec((tm, 2 * LANES), row),
        pl.BlockSpec((C_HEADS, tm, LANES), hrow), pl.BlockSpec((C_HEADS, tm, LANES), hrow),
        pl.BlockSpec((tm, LANES), row), pl.BlockSpec((tm, LANES), row), pl.BlockSpec((tm, LANES), row),
        pl.BlockSpec((tm, LANES), row), pl.BlockSpec((tm, LANES), row))
    return pl.pallas_call(
        _inproj_kernel,
        out_shape=out_shape,
        grid=(n // tm,),
        in_specs=[pl.BlockSpec((tm, d), row), pl.BlockSpec((1, d), const2),
                  pl.BlockSpec(w.shape, const2),
                  pl.BlockSpec((tm, tab.shape[1]), lambda i: (i % nt, 0)),
                  pl.BlockSpec((1, Q_LORA), const2), pl.BlockSpec(wuq.shape, const2),
                  pl.BlockSpec(wuk.shape, const3), pl.BlockSpec((1, KV_LORA), const2)],
        out_specs=out_specs,
        compiler_params=_cparams(1),
        name="inproj",
    )(x, g.reshape(1, d), w, tab, qn.reshape(1, Q_LORA), wuq, wuk, kvn.reshape(1, KV_LORA))


def _rope_table(pos):
    pos = pos.astype(F32)[:, None]

    def cs(half):
        inv = ROPE_THETA ** (-jnp.arange(half, dtype=F32) / half)
        ang = pos * inv[None, :]
        return jnp.cos(ang), jnp.sin(ang)

    n = pos.shape[0]
    c32, s32 = cs(HD // 2)
    c16, s16 = cs(D_ROPE // 2)
    one = lambda w: jnp.ones((n, w), F32)
    zero = lambda w: jnp.zeros((n, w), F32)
    cols = [jnp.concatenate([c32, c32, c32, c32], 1), jnp.concatenate([-s32, s32, -s32, s32], 1),
            jnp.concatenate([c32, c32, one(HD)], 1), jnp.concatenate([-s32, s32, zero(HD)], 1),
            jnp.concatenate([c16, c16, one(LANES - D_ROPE)], 1),
            jnp.concatenate([-s16, s16, zero(LANES - D_ROPE)], 1)]
    return jnp.concatenate(cols, axis=1)


def _layout_w_in(w):
    sizes = (A_HEADS * HD, HD, HD, IDX_HEADS * IDX_DIM, IDX_HEADS, IDX_DIM, Q_LORA, KV_LORA, D_ROPE,
             C_HEADS * HD, HD, HD, HD, HD, HD, HD, C_HEADS * N_GATES)
    offs = np.concatenate([[0], np.cumsum(sizes)])
    (a_q, a_k, a_v, a_iq, a_iw, a_ik, b_cq, b_ckv, b_kr,
     c_q, c_kc, c_vc, c_ks, c_vs, c_kw, c_vw, c_g) = [w[:, offs[i]:offs[i + 1]] for i in range(len(sizes))]
    z = lambda k: jnp.zeros((w.shape[0], k), w.dtype)
    misc_pad = LANES - IDX_DIM - IDX_HEADS - C_HEADS * N_GATES
    out = jnp.concatenate([a_q, a_k, a_v, a_iq, a_ik, a_iw, c_g, z(misc_pad), b_cq, b_ckv,
                           b_kr, z(LANES - D_ROPE), c_q, c_kc, c_vc, c_ks, c_vs, c_kw, c_vw], axis=1)
    assert out.shape[1] == C_END
    return _mx(out)


def _layout_mla(w_uq, w_uk, w_uv):
    d_qk = D_NOPE + D_ROPE
    uq = w_uq.reshape(Q_LORA, B_HEADS, d_qk)
    nope = uq[:, :, :D_NOPE].reshape(Q_LORA, B_HEADS * D_NOPE)
    rope = jnp.pad(uq[:, :, D_NOPE:], ((0, 0), (0, 0), (0, LANES - D_ROPE))).reshape(Q_LORA, B_HEADS * LANES)
    wuq = _mx(jnp.concatenate([nope, rope], axis=1))
    ukt = jnp.transpose(w_uk, (1, 2, 0))
    uvh = jnp.transpose(w_uv, (1, 0, 2))
    zk = jnp.zeros((D_NOPE, KV_LORA), w_uk.dtype)
    zv = jnp.zeros((KV_LORA, D_V), w_uv.dtype)
    wuk = jnp.stack([jnp.block([[ukt[2 * p], zk], [zk, ukt[2 * p + 1]]]) for p in range(B_HEADS // 2)])
    wuv = jnp.stack([jnp.block([[uvh[2 * p], zv], [zv, uvh[2 * p + 1]]]) for p in range(B_HEADS // 2)])
    return wuq, _mx(wuk), _mx(wuv)


def _dsa_prompt_kernel(n_keep, qi_ref, misc_ref, aq_ref, kidx_ref, akv_ref, o_ref,
                       keys_ref, m_ref, l_ref, acc_ref):
    i = pl.program_id(1)
    tq = QBLOCK
    kc = keys_ref.shape[2]
    nch = (i * tq) // kc + 1
    qpos = i * tq + _row((tq, kc))
    qi = qi_ref[...].reshape(IDX_HEADS * tq, LANES)
    misc = misc_ref[...]

    def score_chunk(c, carry):
        k = kidx_ref[pl.ds(pl.multiple_of(c * kc, kc), kc), :]
        rel = jnp.maximum(_dot_nt(qi, k), 0.0)
        score = jnp.zeros((tq, kc), F32)
        for hh in range(IDX_HEADS):
            score = score + rel[hh * tq:(hh + 1) * tq] * misc[:, MISC_W + hh:MISC_W + hh + 1]
        valid = (c * kc + _lane((tq, kc))) <= qpos
        keys_ref[c] = jnp.where(valid, _order_key(score), INT_MIN)
        return carry

    lax.fori_loop(0, nch, score_chunk, 0)

    def count_ge(cand):
        def body(c, cnt):
            ge = jnp.where(keys_ref[c] >= cand, 1, 0)
            for j in range(kc // LANES):
                cnt = cnt + ge[:, j * LANES:(j + 1) * LANES]
            return cnt
        cnt = lax.fori_loop(0, nch, body, jnp.zeros((tq, LANES), I32))
        return jnp.sum(cnt, axis=-1, keepdims=True)

    thr = jnp.maximum(_kth_largest(count_ge, n_keep, (tq, 1)), INT_MIN + 1)

    @pl.when(jnp.max(count_ge(thr)) > n_keep)
    def _():
        need = n_keep - count_ge(thr + 1)

        def count_tie_below(cand):
            def body(c, cnt):
                hit = jnp.where((keys_ref[c] == thr) & ((c * kc + _lane((tq, kc))) < cand), 1, 0)
                for j in range(kc // LANES):
                    cnt = cnt + hit[:, j * LANES:(j + 1) * LANES]
                return cnt
            cnt = lax.fori_loop(0, nch, body, jnp.zeros((tq, LANES), I32))
            return jnp.sum(cnt, axis=-1, keepdims=True)

        cut = _tie_cut(count_tie_below, need, int(kidx_ref.shape[0]).bit_length(), (tq, 1))

        def drop(c, carry):
            kk = keys_ref[c]
            keys_ref[c] = jnp.where((kk == thr) & ((c * kc + _lane((tq, kc))) >= cut), INT_MIN, kk)
            return carry

        lax.fori_loop(0, nch, drop, 0)

    m_ref[...] = jnp.full_like(m_ref, NEG)
    l_ref[...] = jnp.zeros_like(l_ref)
    acc_ref[...] = jnp.zeros_like(acc_ref)
    qa = aq_ref[...].reshape(A_HEADS * tq, LANES)

    def attn_chunk(c, carry):
        kv = akv_ref[pl.ds(pl.multiple_of(c * kc, kc), kc), :]
        mask = _tile_rows(keys_ref[c] >= thr, A_HEADS)
        s = jnp.where(mask, _dot_nt(qa, kv), NEG)
        _softmax_update(s, mask, m_ref, l_ref, acc_ref, kv)
        return carry

    lax.fori_loop(0, nch, attn_chunk, 0)
    o = acc_ref[...] / l_ref[...]
    for hh in range(A_HEADS):
        o_ref[:, hh * HD:(hh + 1) * HD] = o[hh * tq:(hh + 1) * tq, HD:].astype(o_ref.dtype)


def _dsa_prompt(aiq, misc, aq, aikb, akvb, b, t):
    n = b * t
    nq = t // QBLOCK
    kc = min(KCHUNK, t)
    n_keep = min(DSA_TOPK, t // 4)
    qmap = lambda bb, i: (0, bb * nq + i, 0)
    return pl.pallas_call(
        functools.partial(_dsa_prompt_kernel, n_keep),
        out_shape=jax.ShapeDtypeStruct((n, A_HEADS * HD), MXU_DTYPE),
        grid=(b, nq),
        in_specs=[pl.BlockSpec((IDX_HEADS, QBLOCK, LANES), qmap),
                  pl.BlockSpec((QBLOCK, LANES), lambda bb, i: (bb * nq + i, 0)),
                  pl.BlockSpec((A_HEADS, QBLOCK, LANES), qmap),
                  pl.BlockSpec((t, LANES), lambda bb, i: (bb, 0)),
                  pl.BlockSpec((t, LANES), lambda bb, i: (bb, 0))],
        out_specs=pl.BlockSpec((QBLOCK, A_HEADS * HD), lambda bb, i: (bb * nq + i, 0)),
        scratch_shapes=[pltpu.VMEM((t // kc, QBLOCK, kc), I32),
                        pltpu.VMEM((A_HEADS * QBLOCK, 1), F32), pltpu.VMEM((A_HEADS * QBLOCK, 1), F32),
                        pltpu.VMEM((A_HEADS * QBLOCK, LANES), F32)],
        compiler_params=_cparams(2),
        name="dsa_prompt",
    )(aiq, misc, aq, aikb, akvb)


def _mla_prompt_kernel(q_ref, kb_ref, wuv_ref, o_ref, m_ref, l_ref, acc_ref):
    i = pl.program_id(1)
    tq = QBLOCK
    kc = min(KCHUNK, kb_ref.shape[0])
    q = q_ref[...].reshape(B_HEADS * tq, 2 * LANES)
    m_ref[...] = jnp.full_like(m_ref, NEG)
    l_ref[...] = jnp.zeros_like(l_ref)
    acc_ref[...] = jnp.zeros_like(acc_ref)

    def chunk(c, masked):
        k = kb_ref[pl.ds(pl.multiple_of(c * kc, kc), kc), :]
        s = _dot_nt(q, k)
        mask = None
        if masked:
            mask = _tile_rows((c * kc + _lane((tq, kc))) <= (i * tq + _row((tq, kc))), B_HEADS)
            s = jnp.where(mask, s, NEG)
        _softmax_update(s, mask, m_ref, l_ref, acc_ref, k[:, :LANES])

    nfull = (i * tq) // kc

    def body(c, carry):
        chunk(c, False)
        return carry

    lax.fori_loop(0, nfull, body, 0)
    chunk(nfull, True)
    o = _mx(acc_ref[...] / l_ref[...])
    for p in range(B_HEADS // 2):
        pair = jnp.concatenate([o[2 * p * tq:(2 * p + 1) * tq], o[(2 * p + 1) * tq:(2 * p + 2) * tq]], axis=1)
        o_ref[:, p * LANES:(p + 1) * LANES] = _dot(pair, wuv_ref[p]).astype(o_ref.dtype)


def _mla_prompt(qcat, kb, wuv, b, t):
    n = b * t
    nq = t // QBLOCK
    return pl.pallas_call(
        _mla_prompt_kernel,
        out_shape=jax.ShapeDtypeStruct((n, B_HEADS * D_V), MXU_DTYPE),
        grid=(b, nq),
        in_specs=[pl.BlockSpec((B_HEADS, QBLOCK, 2 * LANES), lambda bb, i: (0, bb * nq + i, 0)),
                  pl.BlockSpec((t, 2 * LANES), lambda bb, i: (bb, 0)),
                  pl.BlockSpec(wuv.shape, lambda bb, i: (0, 0, 0))],
        out_specs=pl.BlockSpec((QBLOCK, B_HEADS * D_V), lambda bb, i: (bb * nq + i, 0)),
        scratch_shapes=[pltpu.VMEM((B_HEADS * QBLOCK, 1), F32), pltpu.VMEM((B_HEADS * QBLOCK, 1), F32),
                        pltpu.VMEM((B_HEADS * QBLOCK, LANES), F32)],
        compiler_params=_cparams(2),
        name="mla_prompt",
    )(qcat, kb, wuv)


def _chunk_sums(x, wp):
    x3 = x.reshape(x.shape[0] // CMP_STRIDE, CMP_STRIDE, LANES)
    p0 = jnp.sum(x3 * wp[None, :CMP_STRIDE], axis=1)
    p1 = jnp.sum(x3 * wp[None, CMP_STRIDE:], axis=1)
    return jnp.concatenate([p0, p1], axis=1)


def _cmp_rows_kernel(x_ref, wp_ref, p_ref):
    p_ref[...] = _chunk_sums(x_ref[...], wp_ref[...])


def _cmp_rows(rows, wp, b, t):
    tr = min(1024, t)
    return pl.pallas_call(
        _cmp_rows_kernel,
        out_shape=jax.ShapeDtypeStruct((b * t // CMP_STRIDE, 2 * LANES), F32),
        grid=(b * t // tr,),
        in_specs=[pl.BlockSpec((tr, LANES), lambda i: (i, 0)), pl.BlockSpec(wp.shape, lambda i: (0, 0))],
        out_specs=pl.BlockSpec((tr // CMP_STRIDE, 2 * LANES), lambda i: (i, 0)),
        compiler_params=_cparams(1),
        name="cmp_rows",
    )(rows, wp)


def _cmp_pages_kernel(npg, pt_ref, *refs):
    pages, wp_ref, p_ref = refs[:npg], refs[npg], refs[npg + 1]
    cpp = PAGE // CMP_STRIDE
    for g in range(npg):
        p_ref[g * cpp:(g + 1) * cpp, :] = _chunk_sums(pages[g][...], wp_ref[...])


def _page_specs(npg, width, layer, n_pages):
    def spec(g):
        return pl.BlockSpec((None, None, PAGE, width),
                            lambda bb, s, pt: (layer, pt[bb * n_pages + s * npg + g], 0, 0))
    return [spec(g) for g in range(npg)]


def _cmp_pages(pool, layer, pt, wp, b, n_pages):
    npg = min(PAGES_PER_STEP, n_pages)
    cpp = PAGE // CMP_STRIDE
    grid_spec = pltpu.PrefetchScalarGridSpec(
        num_scalar_prefetch=1, grid=(b, n_pages // npg),
        in_specs=_page_specs(npg, LANES, layer, n_pages) + [pl.BlockSpec(wp.shape, lambda bb, s, pt: (0, 0))],
        out_specs=pl.BlockSpec((None, npg * cpp, 2 * LANES), lambda bb, s, pt: (bb, s, 0)))
    return pl.pallas_call(
        functools.partial(_cmp_pages_kernel, npg),
        out_shape=jax.ShapeDtypeStruct((b, n_pages * cpp, 2 * LANES), F32),
        grid_spec=grid_spec,
        compiler_params=_cparams(2),
        name="cmp_pages",
    )(pt, *([pool] * npg), wp)


def _cmp_finish_kernel(n_cmp, p0_ref, p1_ref, w_ref, o_ref):
    blk = p0_ref[:, :LANES] + p1_ref[...]
    out = _dot(_mx(blk), w_ref[...])
    o_ref[...] = jnp.where(_row(out.shape) < n_cmp, out, 0.0).astype(o_ref.dtype)


def _cmp_finish(p, wproj, n_cmp):
    b, n_chunk, _ = p.shape
    p1s = jnp.pad(p[:, 1:, LANES:], ((0, 0), (0, 1), (0, 0)))
    return pl.pallas_call(
        functools.partial(_cmp_finish_kernel, n_cmp),
        out_shape=jax.ShapeDtypeStruct((b, n_chunk, LANES), MXU_DTYPE),
        grid=(b,),
        in_specs=[pl.BlockSpec((None, n_chunk, 2 * LANES), lambda bb: (bb, 0, 0)),
                  pl.BlockSpec((None, n_chunk, LANES), lambda bb: (bb, 0, 0)),
                  pl.BlockSpec(wproj.shape, lambda bb: (0, 0))],
        out_specs=pl.BlockSpec((None, n_chunk, LANES), lambda bb: (bb, 0, 0)),
        compiler_params=_cparams(1),
        name="cmp_finish",
    )(p, p1s, wproj)


def _band_matrix(n_rows, n_cols):
    r = SLC_BLOCK // CMP_STRIDE
    c = CMP_LEN // CMP_STRIDE
    i = np.arange(n_rows)[:, None]
    j = np.arange(n_cols)[None, :]
    return jnp.asarray((i >= r * j - (c - 1)) & (i <= r * j + r - 1), MXU_DTYPE)


def _importance(pc, band):
    hi = pc.astype(jnp.bfloat16)
    r1 = pc - hi.astype(F32)
    mid = r1.astype(jnp.bfloat16)
    lo = (r1 - mid.astype(F32)).astype(jnp.bfloat16)
    band = band.astype(jnp.bfloat16)
    return _dot(hi, band) + _dot(mid, band) + _dot(lo, band)


def _select_blocks(imp, qpos, n_slc, n_top):
    j = _lane(imp.shape)
    cur = qpos // SLC_BLOCK
    forced = (j == 0) | (j == cur) | (j == cur - 1)
    adm = (j * SLC_BLOCK <= qpos) & (j < n_slc)
    score = jnp.where(forced, FORCE_SCORE, jnp.where(adm, imp, -jnp.inf))
    key = jnp.where(adm, _order_key(score), INT_MIN)
    return jnp.where(_topk_mask(key, j, n_top, int(imp.shape[1]).bit_length()), 1.0, 0.0)


def _masked_softmax(s, mask):
    s = jnp.where(mask, s, NEG)
    m = jnp.max(s, axis=-1, keepdims=True)
    e = jnp.where(mask, jnp.exp(s - m), 0.0)
    return e / jnp.maximum(jnp.sum(e, axis=-1, keepdims=True), jnp.finfo(F32).tiny)


def _nsa_prompt_kernel(n_cmp, n_slc, n_top, cq_ref, cqr_ref, misc_ref, kvc_ref, band_ref,
                       slc_ref, win_ref, o_ref, m_ref, l_ref, acc_ref):
    i = pl.program_id(1)
    tq = QBLOCK
    t = slc_ref.shape[0]
    kc = min(KCHUNK, t)
    q = cq_ref[...].reshape(C_HEADS * tq, LANES)
    qr = cqr_ref[...].reshape(C_HEADS * tq, LANES)

    kvc = kvc_ref[...]
    ncp = kvc.shape[0]
    qpos_c = i * tq + _row((tq, ncp))
    n_id = _lane((tq, ncp))
    cmask = (n_id * CMP_STRIDE + CMP_LEN - 1 <= qpos_c) & (n_id < n_cmp)
    p_cmp = _masked_softmax(_dot_nt(q, kvc), _tile_rows(cmask, C_HEADS))
    o_cmp = _dot(_mx(p_cmp), kvc)
    pc = p_cmp[0:tq]
    for hh in range(1, C_HEADS):
        pc = pc + p_cmp[hh * tq:(hh + 1) * tq]

    imp = _importance(pc, band_ref[...])
    sel = _select_blocks(imp, i * tq + _row(imp.shape), n_slc, n_top).astype(jnp.bfloat16)

    m_ref[...] = jnp.full_like(m_ref, NEG)
    l_ref[...] = jnp.zeros_like(l_ref)
    acc_ref[...] = jnp.zeros_like(acc_ref)
    nsl = sel.shape[1]
    qpos = i * tq + _row((tq, kc))

    def slc_chunk(c, carry):
        kv = slc_ref[pl.ds(pl.multiple_of(c * kc, kc), kc), :]
        kpos = c * kc + _lane((nsl, kc))
        expand = jnp.where(_row((nsl, kc)) == kpos // SLC_BLOCK, 1.0, 0.0).astype(jnp.bfloat16)
        picked = _dot(sel, expand) > 0.5
        mask = _tile_rows(picked & ((c * kc + _lane((tq, kc))) <= qpos), C_HEADS)
        s = jnp.where(mask, _dot_nt(qr, kv), NEG)
        _softmax_update(s, mask, m_ref, l_ref, acc_ref, kv)
        return carry

    lax.fori_loop(0, (i * tq) // kc + 1, slc_chunk, 0)
    o_slc = acc_ref[...] / jnp.maximum(l_ref[...], jnp.finfo(F32).tiny)

    nw = min(WINDOW + tq, t)
    start = jnp.maximum(i * tq + tq - nw, 0)
    rows = win_ref[pl.ds(pl.multiple_of(start, tq), nw), :]
    dist = (i * tq + _row((tq, nw))) - (start + _lane((tq, nw)))
    wmask = _tile_rows((dist >= 0) & (dist < WINDOW), C_HEADS)
    o_win = _dot(_mx(_masked_softmax(_dot_nt(qr, rows), wmask)), rows)

    misc = misc_ref[...]
    for hh in range(C_HEADS):
        g = [misc[:, MISC_G + N_GATES * hh + k:MISC_G + N_GATES * hh + k + 1] for k in range(N_GATES)]
        sl = slice(hh * tq, (hh + 1) * tq)
        o = g[0] * o_cmp[sl] + g[1] * o_slc[sl] + g[2] * o_win[sl]
        o_ref[:, hh * HD:(hh + 1) * HD] = o[:, HD:].astype(o_ref.dtype)


def _nsa_prompt(cq, cqr, misc, kvc, slcb, winb, b, t):
    n = b * t
    nq = t // QBLOCK
    n_chunk = t // CMP_STRIDE
    n_cmp = n_chunk - CMP_LEN // CMP_STRIDE + 1
    n_slc = -(-t // SLC_BLOCK)
    n_top = min(N_SLC, n_slc)
    nsl = -(-n_slc // LANES) * LANES
    band = _band_matrix(n_chunk, nsl)
    qmap = lambda bb, i: (0, bb * nq + i, 0)
    return pl.pallas_call(
        functools.partial(_nsa_prompt_kernel, n_cmp, n_slc, n_top),
        out_shape=jax.ShapeDtypeStruct((n, C_HEADS * HD), MXU_DTYPE),
        grid=(b, nq),
        in_specs=[pl.BlockSpec((C_HEADS, QBLOCK, LANES), qmap),
                  pl.BlockSpec((C_HEADS, QBLOCK, LANES), qmap),
                  pl.BlockSpec((QBLOCK, LANES), lambda bb, i: (bb * nq + i, 0)),
                  pl.BlockSpec((None, n_chunk, LANES), lambda bb, i: (bb, 0, 0)),
                  pl.BlockSpec(band.shape, lambda bb, i: (0, 0)),
                  pl.BlockSpec((t, LANES), lambda bb, i: (bb, 0)),
                  pl.BlockSpec((t, LANES), lambda bb, i: (bb, 0))],
        out_specs=pl.BlockSpec((QBLOCK, C_HEADS * HD), lambda bb, i: (bb * nq + i, 0)),
        scratch_shapes=[pltpu.VMEM((C_HEADS * QBLOCK, 1), F32), pltpu.VMEM((C_HEADS * QBLOCK, 1), F32),
                        pltpu.VMEM((C_HEADS * QBLOCK, LANES), F32)],
        compiler_params=_cparams(2),
        name="nsa_prompt",
    )(cq, cqr, misc, kvc, band, slcb, winb)


def _stack_heads(ref):
    x = ref[...]
    return _mx(x.reshape(x.shape[0] * x.shape[1], x.shape[2]))


def _new_mask(shape, dec):
    return (_lane(shape) <= _row(shape) % dec) & (_lane(shape) < dec)


def _dsa_scores_kernel(npg, pt_ref, *refs):
    qi_ref, misc_ref = refs[0], refs[1]
    pages = refs[2:2 + npg]
    newk_ref, scp_ref, scn_ref, kbuf = refs[2 + npg:6 + npg]
    dec = misc_ref.shape[0]
    q = _stack_heads(qi_ref)[:, :IDX_DIM]
    misc = misc_ref[...]

    def weighted(rel):
        score = jnp.zeros((dec, rel.shape[1]), F32)
        for hh in range(IDX_HEADS):
            score = score + rel[hh * dec:(hh + 1) * dec] * misc[:, MISC_W + hh:MISC_W + hh + 1]
        return score

    for g in range(npg):
        kbuf[g * PAGE:(g + 1) * PAGE, :] = _mx(pages[g][...])
    scp_ref[...] = weighted(jnp.maximum(_dot_nt(q, kbuf[...]), 0.0))

    @pl.when(pl.program_id(1) == 0)
    def _():
        sc = weighted(jnp.maximum(_dot_nt(q, newk_ref[:, :IDX_DIM]), 0.0))
        scn_ref[...] = jnp.where(_new_mask(sc.shape, dec), sc, -jnp.inf)


def _dsa_scores(aiq, misc, pool, layer, pt, newk, b, dec, n_pages):
    npg = min(PAGES_PER_STEP, n_pages)
    grid_spec = pltpu.PrefetchScalarGridSpec(
        num_scalar_prefetch=1, grid=(b, n_pages // npg),
        in_specs=[pl.BlockSpec((IDX_HEADS, dec, LANES), lambda bb, s, pt: (0, bb, 0)),
                  pl.BlockSpec((dec, LANES), lambda bb, s, pt: (bb, 0))]
        + _page_specs(npg, IDX_DIM, layer, n_pages)
        + [pl.BlockSpec((None, PAGE, LANES), lambda bb, s, pt: (bb, 0, 0))],
        out_specs=(pl.BlockSpec((None, dec, npg * PAGE), lambda bb, s, pt: (bb, 0, s)),
                   pl.BlockSpec((None, dec, LANES), lambda bb, s, pt: (bb, 0, 0))),
        scratch_shapes=[pltpu.VMEM((npg * PAGE, IDX_DIM), MXU_DTYPE)])
    return pl.pallas_call(
        functools.partial(_dsa_scores_kernel, npg),
        out_shape=(jax.ShapeDtypeStruct((b, dec, n_pages * PAGE), F32),
                   jax.ShapeDtypeStruct((b, dec, LANES), F32)),
        grid_spec=grid_spec,
        compiler_params=_cparams(2),
        name="dsa_scores",
    )(pt, aiq, misc, *([pool] * npg), newk)


def _dsa_sample_kernel(npg, n_keep, pt_ref, *refs):
    aq_ref, scfull_ref, scn_ref, scstep_ref = refs[:4]
    pages = refs[4:4 + npg]
    newkv_ref, o_ref, thr_ref, cut_ref, m_ref, l_ref, acc_ref, kvbuf = refs[4 + npg:12 + npg]
    dec = scn_ref.shape[0]
    past_len = scfull_ref.shape[1]
    s_id = pl.program_id(1)
    q = _stack_heads(aq_ref)

    def picked(key, pos):
        thr = thr_ref[:, 0:1]
        return (key > thr) | ((key == thr) & (pos < cut_ref[:, 0:1]))

    @pl.when(s_id == 0)
    def _():
        new_ok = _new_mask((dec, LANES), dec)
        key_new = jnp.where(new_ok, _order_key(scn_ref[...]), INT_MIN)
        key_all = _order_key(scfull_ref[...])
        pos_all = _lane(key_all.shape)
        pos_new = past_len + _lane(key_new.shape)

        def count(pred_all, pred_new):
            return (jnp.sum(jnp.where(pred_all, 1, 0), axis=-1, keepdims=True)
                    + jnp.sum(jnp.where(pred_new, 1, 0), axis=-1, keepdims=True))

        thr = jnp.maximum(_kth_largest(lambda c: count(key_all >= c, key_new >= c), n_keep, (dec, 1)),
                          INT_MIN + 1)
        need = n_keep - count(key_all > thr, key_new > thr)
        cut = _tie_cut(lambda c: count((key_all == thr) & (pos_all < c), (key_new == thr) & (pos_new < c)),
                       need, int(past_len + LANES).bit_length(), (dec, 1))
        thr_ref[...] = jnp.broadcast_to(thr, thr_ref.shape)
        cut_ref[...] = jnp.broadcast_to(cut, cut_ref.shape)
        m_ref[...] = jnp.full_like(m_ref, NEG)
        l_ref[...] = jnp.zeros_like(l_ref)
        acc_ref[...] = jnp.zeros_like(acc_ref)
        kv = newkv_ref[...]
        mask = _tile_rows(picked(key_new, pos_new), A_HEADS)
        _softmax_update(jnp.where(mask, _dot_nt(q, kv), NEG), mask, m_ref, l_ref, acc_ref, kv)

    for g in range(npg):
        kvbuf[g * PAGE:(g + 1) * PAGE, :] = _mx(pages[g][...])
    kv = kvbuf[...]
    key_step = _order_key(scstep_ref[...])
    mask = _tile_rows(picked(key_step, s_id * (npg * PAGE) + _lane(key_step.shape)), A_HEADS)
    _softmax_update(jnp.where(mask, _dot_nt(q, kv), NEG), mask, m_ref, l_ref, acc_ref, kv)

    @pl.when(s_id == pl.num_programs(1) - 1)
    def _():
        o = acc_ref[...] / l_ref[...]
        for hh in range(A_HEADS):
            o_ref[:, hh * HD:(hh + 1) * HD] = o[hh * dec:(hh + 1) * dec, HD:]


def _dsa_sample(aq, scp, scn, pool, layer, pt, newkv, b, dec, n_pages):
    npg = min(PAGES_PER_STEP, n_pages)
    n_keep = min(DSA_TOPK, (n_pages * PAGE + dec) // 4)
    grid_spec = pltpu.PrefetchScalarGridSpec(
        num_scalar_prefetch=1, grid=(b, n_pages // npg),
        in_specs=[pl.BlockSpec((A_HEADS, dec, LANES), lambda bb, s, pt: (0, bb, 0)),
                  pl.BlockSpec((None, dec, n_pages * PAGE), lambda bb, s, pt: (bb, 0, 0)),
                  pl.BlockSpec((None, dec, LANES), lambda bb, s, pt: (bb, 0, 0)),
                  pl.BlockSpec((None, dec, npg * PAGE), lambda bb, s, pt: (bb, 0, s))]
        + _page_specs(npg, 2 * HD, layer, n_pages)
        + [pl.BlockSpec((None, PAGE, LANES), lambda bb, s, pt: (bb, 0, 0))],
        out_specs=pl.BlockSpec((dec, A_HEADS * HD), lambda bb, s, pt: (bb, 0)),
        scratch_shapes=[pltpu.VMEM((dec, LANES), I32), pltpu.VMEM((dec, LANES), I32),
                        pltpu.VMEM((A_HEADS * dec, 1), F32), pltpu.VMEM((A_HEADS * dec, 1), F32),
                        pltpu.VMEM((A_HEADS * dec, LANES), F32),
                        pltpu.VMEM((npg * PAGE, LANES), MXU_DTYPE)])
    return pl.pallas_call(
        functools.partial(_dsa_sample_kernel, npg, n_keep),
        out_shape=jax.ShapeDtypeStruct((b * dec, A_HEADS * HD), F32),
        grid_spec=grid_spec,
        compiler_params=_cparams(2),
        name="dsa_sample",
    )(pt, aq, scp, scn, scp, *([pool] * npg), newkv)


def _mla_sample_kernel(npg, pt_ref, *refs):
    q_ref = refs[0]
    pages = refs[1:1 + npg]
    newkb_ref, wuv_ref, o_ref, m_ref, l_ref, acc_ref, kbuf = refs[1 + npg:8 + npg]
    dec = o_ref.shape[0]
    s_id = pl.program_id(1)
    q = _stack_heads(q_ref)
    wk = KV_LORA + D_ROPE

    @pl.when(s_id == 0)
    def _():
        kbuf[...] = jnp.zeros_like(kbuf)
        m_ref[...] = jnp.full_like(m_ref, NEG)
        l_ref[...] = jnp.zeros_like(l_ref)
        acc_ref[...] = jnp.zeros_like(acc_ref)
        k = newkb_ref[...]
        mask = _new_mask((B_HEADS * dec, PAGE), dec)
        _softmax_update(jnp.where(mask, _dot_nt(q, k), NEG), mask, m_ref, l_ref, acc_ref, k[:, :LANES])

    for g in range(npg):
        kbuf[g * PAGE:(g + 1) * PAGE, 0:wk] = _mx(pages[g][...])
    k = kbuf[...]
    _softmax_update(_dot_nt(q, k), None, m_ref, l_ref, acc_ref, k[:, :LANES])

    @pl.when(s_id == pl.num_programs(1) - 1)
    def _():
        o = _mx(acc_ref[...] / l_ref[...])
        for p in range(B_HEADS // 2):
            pair = jnp.concatenate([o[2 * p * dec:(2 * p + 1) * dec], o[(2 * p + 1) * dec:(2 * p + 2) * dec]], axis=1)
            o_ref[:, p * LANES:(p + 1) * LANES] = _dot(pair, wuv_ref[p])


def _mla_sample(qcat, pool, layer, pt, newkb, wuv, b, dec, n_pages):
    npg = min(PAGES_PER_STEP, n_pages)
    grid_spec = pltpu.PrefetchScalarGridSpec(
        num_scalar_prefetch=1, grid=(b, n_pages // npg),
        in_specs=[pl.BlockSpec((B_HEADS, dec, 2 * LANES), lambda bb, s, pt: (0, bb, 0))]
        + _page_specs(npg, KV_LORA + D_ROPE, layer, n_pages)
        + [pl.BlockSpec((None, PAGE, 2 * LANES), lambda bb, s, pt: (bb, 0, 0)),
           pl.BlockSpec(wuv.shape, lambda bb, s, pt: (0, 0, 0))],
        out_specs=pl.BlockSpec((dec, B_HEADS * D_V), lambda bb, s, pt: (bb, 0)),
        scratch_shapes=[pltpu.VMEM((B_HEADS * dec, 1), F32), pltpu.VMEM((B_HEADS * dec, 1), F32),
                        pltpu.VMEM((B_HEADS * dec, LANES), F32),
                        pltpu.VMEM((npg * PAGE, 2 * LANES), MXU_DTYPE)])
    return pl.pallas_call(
        functools.partial(_mla_sample_kernel, npg),
        out_shape=jax.ShapeDtypeStruct((b * dec, B_HEADS * D_V), F32),
        grid_spec=grid_spec,
        compiler_params=_cparams(2),
        name="mla_sample",
    )(pt, qcat, *([pool] * npg), newkb, wuv)


def _nsa_select_kernel(n_cmp, n_slc, n_top, past_len, bps, cq_ref, cqr_ref, kvc_ref, band_ref,
                       state_ref, neww_ref, newwb_ref, ocw_ref, sel_ref, wstate_ref):
    dec = neww_ref.shape[0]
    q = _stack_heads(cq_ref)
    qr = _stack_heads(cqr_ref)
    rows_q = C_HEADS * dec

    kvc = kvc_ref[...]
    ncp = kvc.shape[0]
    qpos_c = past_len + _row((dec, ncp)) % dec
    n_id = _lane((dec, ncp))
    cmask = (n_id * CMP_STRIDE + CMP_LEN - 1 <= qpos_c) & (n_id < n_cmp)
    p_cmp = _masked_softmax(_dot_nt(q, kvc), _tile_rows(cmask, C_HEADS))
    ocw_ref[0] = _dot(_mx(p_cmp), kvc)
    pc = p_cmp[0:dec]
    for hh in range(1, C_HEADS):
        pc = pc + p_cmp[hh * dec:(hh + 1) * dec]

    imp = _importance(pc, band_ref[...])
    sel = _select_blocks(imp, past_len + _row(imp.shape), n_slc, n_top)
    for s in range(sel_ref.shape[0]):
        tile = sel[:, (s * bps // LANES) * LANES:(s * bps // LANES + 1) * LANES]
        shift = (s * bps) % LANES
        sel_ref[s] = pltpu.roll(tile, LANES - shift, 1) if shift else tile

    state = state_ref[...]
    nst = state.shape[0]
    sb = _mx(state)
    nb = newwb_ref[...]
    trow = _row((rows_q, nst)) % dec
    dist = nst + trow - _lane((rows_q, nst))
    m1 = (dist >= 0) & (dist < WINDOW)
    m2 = _new_mask((rows_q, PAGE), dec)
    s1 = jnp.where(m1, _dot_nt(qr, sb), NEG)
    s2 = jnp.where(m2, _dot_nt(qr, nb), NEG)
    m = jnp.maximum(jnp.max(s1, axis=-1, keepdims=True), jnp.max(s2, axis=-1, keepdims=True))
    e1 = jnp.where(m1, jnp.exp(s1 - m), 0.0)
    e2 = jnp.where(m2, jnp.exp(s2 - m), 0.0)
    den = jnp.maximum(jnp.sum(e1, axis=-1, keepdims=True) + jnp.sum(e2, axis=-1, keepdims=True),
                      jnp.finfo(F32).tiny)
    ocw_ref[1] = _dot(_mx(e1 / den), sb) + _dot(_mx(e2 / den), nb)

    wstate_ref[0:nst - dec, :] = state[dec:, :]
    wstate_ref[nst - dec:nst, :] = neww_ref[...]


def _nsa_select(cq, cqr, kvc, state, layer, neww, newwb, b, dec, n_pages, npg):
    past_len = n_pages * PAGE
    total = past_len + dec
    n_chunk = kvc.shape[1]
    n_cmp = total // CMP_STRIDE - CMP_LEN // CMP_STRIDE + 1
    n_slc = -(-total // SLC_BLOCK)
    n_top = min(N_SLC, n_slc)
    bps = npg * PAGE // SLC_BLOCK
    nsteps = n_pages // npg
    nsl = -(-max(n_slc, nsteps * bps) // LANES) * LANES
    band = _band_matrix(n_chunk, nsl)
    nst = state.shape[2]
    return pl.pallas_call(
        functools.partial(_nsa_select_kernel, n_cmp, n_slc, n_top, past_len, bps),
        out_shape=(jax.ShapeDtypeStruct((b, 2, C_HEADS * dec, LANES), F32),
                   jax.ShapeDtypeStruct((b, nsteps, dec, LANES), F32),
                   jax.ShapeDtypeStruct((b, nst, LANES), F32)),
        grid=(b,),
        in_specs=[pl.BlockSpec((C_HEADS, dec, LANES), lambda bb: (0, bb, 0)),
                  pl.BlockSpec((C_HEADS, dec, LANES), lambda bb: (0, bb, 0)),
                  pl.BlockSpec((None, n_chunk, LANES), lambda bb: (bb, 0, 0)),
                  pl.BlockSpec(band.shape, lambda bb: (0, 0)),
                  pl.BlockSpec((None, None, nst, LANES), lambda bb: (layer, bb, 0, 0)),
                  pl.BlockSpec((dec, LANES), lambda bb: (bb, 0)),
                  pl.BlockSpec((None, PAGE, LANES), lambda bb: (bb, 0, 0))],
        out_specs=(pl.BlockSpec((None, 2, C_HEADS * dec, LANES), lambda bb: (bb, 0, 0, 0)),
                   pl.BlockSpec((None, nsteps, dec, LANES), lambda bb: (bb, 0, 0, 0)),
                   pl.BlockSpec((None, nst, LANES), lambda bb: (bb, 0, 0))),
        compiler_params=_cparams(1),
        name="nsa_select",
    )(cq, cqr, kvc, band, state, neww, newwb)


def _nsa_slc_kernel(npg, pt_ref, *refs):
    cqr_ref, misc_ref, ocw_ref, sel_ref = refs[:4]
    pages = refs[4:4 + npg]
    newb_ref, o_ref, m_ref, l_ref, acc_ref, kvbuf = refs[4 + npg:10 + npg]
    dec = misc_ref.shape[0]
    s_id = pl.program_id(1)
    qr = _stack_heads(cqr_ref)

    @pl.when(s_id == 0)
    def _():
        m_ref[...] = jnp.full_like(m_ref, NEG)
        l_ref[...] = jnp.zeros_like(l_ref)
        acc_ref[...] = jnp.zeros_like(acc_ref)
        kv = newb_ref[...]
        mask = _new_mask((C_HEADS * dec, PAGE), dec)
        _softmax_update(jnp.where(mask, _dot_nt(qr, kv), NEG), mask, m_ref, l_ref, acc_ref, kv)

    sel = sel_ref[...]
    low = _lane((dec, PAGE)) < SLC_BLOCK
    masks = []
    for g in range(npg):
        kvbuf[g * PAGE:(g + 1) * PAGE, :] = _mx(pages[g][...])
        masks.append(jnp.where(low, sel[:, 2 * g:2 * g + 1], sel[:, 2 * g + 1:2 * g + 2]) > 0.5)
    mask = _tile_rows(jnp.concatenate(masks, axis=1), C_HEADS)
    kv = kvbuf[...]
    _softmax_update(jnp.where(mask, _dot_nt(qr, kv), NEG), mask, m_ref, l_ref, acc_ref, kv)

    @pl.when(s_id == pl.num_programs(1) - 1)
    def _():
        o_slc = acc_ref[...] / jnp.maximum(l_ref[...], jnp.finfo(F32).tiny)
        misc = misc_ref[...]
        o_cmp = ocw_ref[0]
        o_win = ocw_ref[1]
        for hh in range(C_HEADS):
            g = [misc[:, MISC_G + N_GATES * hh + k:MISC_G + N_GATES * hh + k + 1] for k in range(N_GATES)]
            sl = slice(hh * dec, (hh + 1) * dec)
            o = g[0] * o_cmp[sl] + g[1] * o_slc[sl] + g[2] * o_win[sl]
            o_ref[:, hh * HD:(hh + 1) * HD] = o[:, HD:]


def _nsa_slc(cqr, misc, ocw, selsteps, pool, layer, pt, newb, b, dec, n_pages, npg):
    grid_spec = pltpu.PrefetchScalarGridSpec(
        num_scalar_prefetch=1, grid=(b, n_pages // npg),
        in_specs=[pl.BlockSpec((C_HEADS, dec, LANES), lambda bb, s, pt: (0, bb, 0)),
                  pl.BlockSpec((dec, LANES), lambda bb, s, pt: (bb, 0)),
                  pl.BlockSpec((None, 2, C_HEADS * dec, LANES), lambda bb, s, pt: (bb, 0, 0, 0)),
                  pl.BlockSpec((None, None, dec, LANES), lambda bb, s, pt: (bb, s, 0, 0))]
        + _page_specs(npg, 2 * HD, layer, n_pages)
        + [pl.BlockSpec((None, PAGE, LANES), lambda bb, s, pt: (bb, 0, 0))],
        out_specs=pl.BlockSpec((dec, C_HEADS * HD), lambda bb, s, pt: (bb, 0)),
        scratch_shapes=[pltpu.VMEM((C_HEADS * dec, 1), F32), pltpu.VMEM((C_HEADS * dec, 1), F32),
                        pltpu.VMEM((C_HEADS * dec, LANES), F32),
                        pltpu.VMEM((npg * PAGE, LANES), MXU_DTYPE)])
    return pl.pallas_call(
        functools.partial(_nsa_slc_kernel, npg),
        out_shape=jax.ShapeDtypeStruct((b * dec, C_HEADS * HD), F32),
        grid_spec=grid_spec,
        compiler_params=_cparams(2),
        name="nsa_slc",
    )(pt, cqr, misc, ocw, selsteps, *([pool] * npg), newb)


def _new_pages(rows, b, dec):
    r = rows.reshape(b, dec, rows.shape[-1])
    return _mx(jnp.pad(r, ((0, 0), (0, PAGE - dec), (0, 0))))


def _layer_weights(l, p):
    wuq, wuk, wuv = _layout_mla(p['b_w_uq'][l], p['b_w_uk'][l], p['b_w_uv'][l])
    wout = _mx(p['w_out'][l])
    na, nb = A_HEADS * HD, B_HEADS * D_V
    zero = jnp.zeros((HD, HD), F32)
    return dict(
        w_in=_layout_w_in(p['w_in'][l]), wuq=wuq, wuk=wuk, wuv=wuv,
        wout=(wout[:na], wout[na:na + nb], wout[na + nb:]),
        wp=jnp.concatenate([p['c_cmp_pos_k'][l], p['c_cmp_pos_v'][l]], axis=1),
        wproj=_mx(jnp.block([[p['c_cmp_proj_k'][l], zero], [zero, p['c_cmp_proj_v'][l]]])),
        pre13=_mx(p['ffn_pre_w13'][l]), pre2=_mx(p['ffn_pre_w2'][l]),
        post13=_mx(p['ffn_post_w13'][l]), post2=_mx(p['ffn_post_w2'][l]))


def _prompt_mixers(proj, lw, b, t):
    (aq, akv, akvb, aiq, misc, aikb, qcat, bckv, kb, cq, cqr, ccmp, cslc, cslcb, cwin, cwinb) = proj
    o_a = _dsa_prompt(aiq, misc, aq, aikb, akvb, b, t)
    o_b = _mla_prompt(qcat, kb, lw['wuv'], b, t)
    n_chunk = t // CMP_STRIDE
    n_cmp = n_chunk - CMP_LEN // CMP_STRIDE + 1
    psum = _cmp_rows(ccmp, lw['wp'], b, t).reshape(b, n_chunk, 2 * LANES)
    kvc = _cmp_finish(psum, lw['wproj'], n_cmp)
    o_c = _nsa_prompt(cq, cqr, misc, kvc, cslcb, cwinb, b, t)
    nw = min(WINDOW, t)
    rows = (akv.reshape(b, t, -1), misc.reshape(b, t, -1)[..., :IDX_DIM],
            bckv.reshape(b, t, -1)[..., :KV_LORA + D_ROPE], ccmp.reshape(b, t, -1),
            cslc.reshape(b, t, -1), cwin.reshape(b, t, -1)[:, t - nw:])
    return (o_a, o_b, o_c), rows


def _sample_mixers(proj, lw, l, b, dec, pools, state_win, pt, n_pages):
    (aq, akv, akvb, aiq, misc, aikb, qcat, bckv, kb, cq, cqr, ccmp, cslc, cslcb, cwin, cwinb) = proj
    npg = min(PAGES_PER_STEP, n_pages)
    scp, scn = _dsa_scores(aiq, misc, pools['a_kidx'], l, pt, _new_pages(aikb, b, dec), b, dec, n_pages)
    o_a = _dsa_sample(aq, scp, scn, pools['a_kv'], l, pt, _new_pages(akvb, b, dec), b, dec, n_pages)
    o_b = _mla_sample(qcat, pools['b_ckv'], l, pt, _new_pages(kb, b, dec), lw['wuv'], b, dec, n_pages)
    total = n_pages * PAGE + dec
    n_cmp = total // CMP_STRIDE - CMP_LEN // CMP_STRIDE + 1
    psum = _cmp_pages(pools['c_cmp'], l, pt, lw['wp'], b, n_pages)
    kvc = _cmp_finish(psum, lw['wproj'], n_cmp)
    ocw, selsteps, wstate = _nsa_select(cq, cqr, kvc, state_win, l, cwin, _new_pages(cwinb, b, dec),
                                        b, dec, n_pages, npg)
    o_c = _nsa_slc(cqr, misc, ocw, selsteps, pools['c_slc'], l, pt, _new_pages(cslcb, b, dec),
                   b, dec, n_pages, npg)
    rows = (akv.reshape(b, dec, -1), misc.reshape(b, dec, -1)[..., :IDX_DIM],
            bckv.reshape(b, dec, -1)[..., :KV_LORA + D_ROPE], ccmp.reshape(b, dec, -1),
            cslc.reshape(b, dec, -1), wstate)
    return (o_a, o_b, o_c), rows


def _trunk(x, tab, qdt, mixers, weights, p):
    b, t, d = x.shape
    x = x.reshape(b * t, d)
    depth = len(weights)
    per_layer = []
    for l, lw in enumerate(weights):
        x = _ffn(x, p['norm_ffn_pre'][l], lw['pre13'], lw['pre2'])
        proj = _inproj(x, p['norm_mix'][l], lw['w_in'], tab, p['b_q_norm'][l], lw['wuq'], lw['wuk'],
                       p['b_kv_norm'][l], qdt)
        o, rows = mixers(proj, lw, l)
        x = _ffn(x, p['norm_ffn_post'][l], lw['post13'], lw['post2'], mix=tuple(zip(o, lw['wout'])),
                 final_g=p['norm_final'] if l == depth - 1 else None)
        per_layer.append(rows)
    stacked = tuple(jnp.stack(r, axis=0) for r in zip(*per_layer))
    return x.reshape(b, t, d), stacked


def kernel(x_prompt, x_sample, cache_a_kv, cache_a_kidx, cache_b_ckv, cache_c_cmp, cache_c_slc,
           state_c_win, page_table, w_in, w_out, b_w_uq, b_w_uk, b_w_uv, b_q_norm, b_kv_norm,
           c_cmp_pos_k, c_cmp_pos_v, c_cmp_proj_k, c_cmp_proj_v,
           ffn_pre_w13, ffn_pre_w2, ffn_post_w13, ffn_post_w2,
           norm_ffn_pre, norm_mix, norm_ffn_post, norm_final):
    p = {'w_in': w_in, 'w_out': w_out, 'b_w_uq': b_w_uq, 'b_w_uk': b_w_uk, 'b_w_uv': b_w_uv,
         'b_q_norm': b_q_norm, 'b_kv_norm': b_kv_norm,
         'c_cmp_pos_k': c_cmp_pos_k, 'c_cmp_pos_v': c_cmp_pos_v,
         'c_cmp_proj_k': c_cmp_proj_k, 'c_cmp_proj_v': c_cmp_proj_v,
         'ffn_pre_w13': ffn_pre_w13, 'ffn_pre_w2': ffn_pre_w2,
         'ffn_post_w13': ffn_post_w13, 'ffn_post_w2': ffn_post_w2,
         'norm_ffn_pre': norm_ffn_pre, 'norm_mix': norm_mix, 'norm_ffn_post': norm_ffn_post,
         'norm_final': norm_final}
    pools = {'a_kv': cache_a_kv, 'a_kidx': cache_a_kidx, 'b_ckv': cache_b_ckv,
             'c_cmp': cache_c_cmp, 'c_slc': cache_c_slc}
    depth = w_in.shape[0]
    weights = [_layer_weights(l, p) for l in range(depth)]
    bp, tp, _ = x_prompt.shape
    bs, dec, _ = x_sample.shape
    n_pages = page_table.shape[1]
    past_len = n_pages * PAGE
    pt = page_table.reshape(-1).astype(I32)

    tab_p = _rope_table(jnp.arange(tp, dtype=I32))
    tab_s = _rope_table(jnp.tile(past_len + jnp.arange(dec, dtype=I32), bs))

    y_p, rows_p = _trunk(x_prompt, tab_p, MXU_DTYPE,
                         lambda proj, lw, l: _prompt_mixers(proj, lw, bp, tp), weights, p)
    y_s, rows_s = _trunk(x_sample, tab_s, F32,
                         lambda proj, lw, l: _sample_mixers(proj, lw, l, bs, dec, pools, state_c_win, pt, n_pages),
                         weights, p)
    out = [y_p, y_s]
    for rp, rs in zip(rows_p, rows_s):
        out += [rp, rs]
    return tuple(out)
```

```python
import functools

import numpy as np
import jax
import jax.numpy as jnp
from jax import lax
from jax.experimental import pallas as pl
from jax.experimental.pallas import tpu as pltpu

HD = 64
A_HEADS = 4
IDX_HEADS = 8
IDX_DIM = 64
DSA_TOPK = 256
B_HEADS = 8
Q_LORA = 256
KV_LORA = 128
D_NOPE = 64
D_ROPE = 32
D_V = 64
C_HEADS = 4
CMP_LEN = 32
CMP_STRIDE = 16
SLC_BLOCK = 64
N_SLC = 16
WINDOW = 512
N_GATES = 3
ROPE_THETA = 10000.0
NORM_EPS = 1e-6
QBLOCK = 128
FORCE_SCORE = 1e9
PAGE = 128

LANES = 128
MXU_DTYPE = jnp.bfloat16
F32 = jnp.float32
I32 = jnp.int32
NEG = -1e30
INT_MIN = -2147483648
VMEM_LIMIT = 56 * 1024 * 1024
PAGES_PER_STEP = 32
KCHUNK = 512

MISC_W = IDX_DIM
MISC_G = IDX_DIM + IDX_HEADS


def _dot(a, b):
    return jnp.dot(a, b, preferred_element_type=F32)


def _dot_nt(a, b):
    return lax.dot_general(a, b, (((1,), (1,)), ((), ())), preferred_element_type=F32)


def _mx(x):
    return x.astype(MXU_DTYPE)


def _rms(x, g):
    r = lax.rsqrt(jnp.mean(x * x, axis=-1, keepdims=True) + NORM_EPS)
    return x * r * g


def _cparams(n_grid):
    return pltpu.CompilerParams(dimension_semantics=("arbitrary",) * n_grid,
                                vmem_limit_bytes=VMEM_LIMIT)


def _lane(shape):
    return lax.broadcasted_iota(I32, shape, len(shape) - 1)


def _row(shape):
    return lax.broadcasted_iota(I32, shape, len(shape) - 2)


def _order_key(score):
    score = jnp.where(score == 0.0, 0.0, score)
    bits = lax.bitcast_convert_type(score, I32)
    return bits ^ ((bits >> 31) & 0x7FFFFFFF)


def _kth_largest(count_ge, k, shape):
    def body(it, t):
        cand = t ^ jnp.left_shift(jnp.int32(1), 31 - it)
        return jnp.where(count_ge(cand) >= k, cand, t)
    return lax.fori_loop(0, 32, body, jnp.full(shape, INT_MIN, I32))


def _tie_cut(count_tie_below, need, nbits, shape):
    def body(it, cut):
        cand = cut | jnp.left_shift(jnp.int32(1), nbits - 1 - it)
        return jnp.where(count_tie_below(cand) <= need, cand, cut)
    return lax.fori_loop(0, nbits, body, jnp.zeros(shape, I32))


def _topk_mask(key, idx, k, nbits):
    rows = (key.shape[0], 1)

    def count(pred):
        return jnp.sum(jnp.where(pred, 1, 0), axis=-1, keepdims=True)

    thr = jnp.maximum(_kth_largest(lambda cand: count(key >= cand), k, rows), INT_MIN + 1)
    above = key > thr
    tied = key == thr
    need = k - count(above)
    cut = _tie_cut(lambda cand: count(tied & (idx < cand)), need, nbits, rows)
    return above | (tied & (idx < cut))


def _softmax_update(s, mask, m_ref, l_ref, acc_ref, v, v_transposed=False):
    m_old = m_ref[...]
    m_new = jnp.maximum(m_old, jnp.max(s, axis=-1, keepdims=True))
    p = jnp.exp(s - m_new)
    if mask is not None:
        p = jnp.where(mask, p, 0.0)
    alpha = jnp.exp(m_old - m_new)
    l_ref[...] = alpha * l_ref[...] + jnp.sum(p, axis=-1, keepdims=True)
    pv = _dot_nt(_mx(p), v) if v_transposed else _dot(_mx(p), v)
    acc_ref[...] = alpha * acc_ref[...] + pv
    m_ref[...] = m_new


def _flash_chunk_t(q, k, vt, mask, m_ref, l_ref, acc_ref):
    s = _dot_nt(k, q)
    if mask is not None:
        s = jnp.where(mask, s, NEG)
    m_old = m_ref[...]
    m_new = jnp.maximum(m_old, jnp.max(s, axis=0, keepdims=True))
    p = jnp.exp(s - m_new)
    if mask is not None:
        p = jnp.where(mask, p, 0.0)
    alpha = jnp.exp(m_old - m_new)
    l_ref[...] = alpha * l_ref[...] + jnp.sum(p, axis=0, keepdims=True)
    acc_ref[...] = alpha * acc_ref[...] + _dot(vt, _mx(p))
    m_ref[...] = m_new


def _chunked_transpose(x, b, t, kc):
    return jnp.swapaxes(x.reshape(b, t // kc, kc, x.shape[-1]), 2, 3)


def _tile_rows(x, n):
    return jnp.concatenate([x] * n, axis=0)


def _ffn_kernel(n_mix, has_final, *refs):
    x_ref = refs[0]
    mix = refs[1:1 + 2 * n_mix]
    g_ref, w1_ref, w3_ref, w2_ref = refs[1 + 2 * n_mix:5 + 2 * n_mix]
    pos = 5 + 2 * n_mix
    gf_ref = refs[pos] if has_final else None
    pos += 1 if has_final else 0
    out_ref, xs, hs, acc = refs[pos:pos + 4]
    f = pl.program_id(1)

    @pl.when(f == 0)
    def _():
        x = x_ref[...]
        for i in range(n_mix):
            x = x + _dot(_mx(mix[2 * i][...]), mix[2 * i + 1][...])
        xs[...] = x
        hs[...] = _mx(_rms(x, g_ref[...]))
        acc[...] = jnp.zeros_like(acc)

    h = hs[...]
    gate = _dot(h, w1_ref[...])
    up = _dot(h, w3_ref[...])
    acc[...] += _dot(_mx(jax.nn.silu(gate) * up), w2_ref[...])

    @pl.when(f == pl.num_programs(1) - 1)
    def _():
        y = xs[...] + 0.5 * acc[...]
        if has_final:
            y = _rms(y, gf_ref[...])
        out_ref[...] = y


def _ffn(x, g, w13, w2, mix=(), final_g=None):
    n, d = x.shape
    d_ff = w2.shape[0]
    tm = min(1024, n)
    tf = 256
    nf = d_ff // tf
    assert n % tm == 0 and d_ff % tf == 0
    in_specs = [pl.BlockSpec((tm, d), lambda i, f: (i, 0))]
    args = [x]
    for o, w in mix:
        in_specs += [pl.BlockSpec((tm, o.shape[1]), lambda i, f: (i, 0)),
                     pl.BlockSpec(w.shape, lambda i, f: (0, 0))]
        args += [o, w]
    in_specs += [pl.BlockSpec((1, d), lambda i, f: (0, 0)),
                 pl.BlockSpec((d, tf), lambda i, f: (0, f)),
                 pl.BlockSpec((d, tf), lambda i, f: (0, f + nf)),
                 pl.BlockSpec((tf, d), lambda i, f: (f, 0))]
    args += [g.reshape(1, d), w13, w13, w2]
    if final_g is not None:
        in_specs.append(pl.BlockSpec((1, d), lambda i, f: (0, 0)))
        args.append(final_g.reshape(1, d))
    return pl.pallas_call(
        functools.partial(_ffn_kernel, len(mix), final_g is not None),
        out_shape=jax.ShapeDtypeStruct((n, d), F32),
        grid=(n // tm, nf),
        in_specs=in_specs,
        out_specs=pl.BlockSpec((tm, d), lambda i, f: (i, 0)),
        scratch_shapes=[pltpu.VMEM((tm, d), F32), pltpu.VMEM((tm, d), MXU_DTYPE),
                        pltpu.VMEM((tm, d), F32)],
        compiler_params=_cparams(2),
        name="ffn",
    )(*args)


C_AQ, C_AKV, C_AIQ, C_MISC, C_BCQ, C_BCKV, C_BKR, C_CQ, C_CCMP, C_CSLC, C_CWIN, C_END = (
    0, 256, 384, 896, 1024, 1280, 1408, 1536, 1792, 1920, 2048, 2176)
T_HH, T_HV, T_R1 = 0, 2, 4


def _rope(x, tab_ref, kind, half):
    cos = tab_ref[:, kind * LANES:(kind + 1) * LANES]
    ssin = tab_ref[:, (kind + 1) * LANES:(kind + 2) * LANES]
    first = (_lane(x.shape) % (2 * half)) < half
    rot = jnp.where(first, pltpu.roll(x, LANES - half, 1), pltpu.roll(x, half, 1))
    return x * cos + rot * ssin


def _split_heads(r):
    low = _lane(r.shape) < HD
    return jnp.where(low, r, 0.0), jnp.where(low, pltpu.roll(r, HD, 1), 0.0)


def _inproj_kernel(x_ref, g_ref, w_ref, tab_ref, qn_ref, wuq_ref, wuk_ref, kvn_ref,
                   aq_ref, akv_ref, akvb_ref, aiq_ref, misc_ref, aikb_ref, qcat_ref,
                   bckv_ref, kb_ref, cq_ref, cqr_ref, ccmp_ref, cslc_ref, cslcb_ref,
                   cwin_ref, cwinb_ref):
    h = _mx(_rms(x_ref[...], g_ref[...]))

    def proj(c0, c1):
        return _dot(h, w_ref[:, c0:c1])

    qscale = HD ** -0.5
    for i in range(A_HEADS // 2):
        r = _rope(proj(C_AQ + LANES * i, C_AQ + LANES * (i + 1)), tab_ref, T_HH, HD // 2) * qscale
        h0, h1 = _split_heads(r)
        aq_ref[2 * i] = h0.astype(aq_ref.dtype)
        aq_ref[2 * i + 1] = h1.astype(aq_ref.dtype)

    akv = _rope(proj(C_AKV, C_AIQ), tab_ref, T_HV, HD // 2)
    akv_ref[...] = akv
    akvb_ref[...] = _mx(akv)

    iscale = IDX_DIM ** -0.5
    for i in range(IDX_HEADS // 2):
        r = _rope(proj(C_AIQ + LANES * i, C_AIQ + LANES * (i + 1)), tab_ref, T_HH, IDX_DIM // 2) * iscale
        h0, h1 = _split_heads(r)
        aiq_ref[2 * i] = h0.astype(aiq_ref.dtype)
        aiq_ref[2 * i + 1] = h1.astype(aiq_ref.dtype)

    misc = _rope(proj(C_MISC, C_BCQ), tab_ref, T_HV, IDX_DIM // 2)
    lane = _lane(misc.shape)
    misc = jnp.where((lane >= MISC_W) & (lane < MISC_G), misc * IDX_HEADS ** -0.5, misc)
    misc = jnp.where((lane >= MISC_G) & (lane < MISC_G + C_HEADS * N_GATES),
                     jax.nn.sigmoid(misc), misc)
    misc_ref[...] = misc
    aikb_ref[...] = _mx(jnp.where(lane < IDX_DIM, misc, 0.0))

    bscale = (D_NOPE + D_ROPE) ** -0.5
    cqn = _mx(_rms(proj(C_BCQ, C_BCKV), qn_ref[...]))
    for p in range(B_HEADS // 2):
        nope = _mx(_dot(cqn, wuq_ref[:, LANES * p:LANES * (p + 1)]))
        lat = _dot(nope, wuk_ref[p]) * bscale
        qcat_ref[2 * p, :, 0:LANES] = lat[:, :LANES].astype(qcat_ref.dtype)
        qcat_ref[2 * p + 1, :, 0:LANES] = lat[:, LANES:].astype(qcat_ref.dtype)
    for hh in range(B_HEADS):
        c0 = B_HEADS * D_NOPE + LANES * hh
        qr = _rope(_dot(cqn, wuq_ref[:, c0:c0 + LANES]), tab_ref, T_R1, D_ROPE // 2) * bscale
        qcat_ref[hh, :, LANES:2 * LANES] = qr.astype(qcat_ref.dtype)

    lat = _rms(proj(C_BCKV, C_BKR), kvn_ref[...])
    kr = _rope(proj(C_BKR, C_CQ), tab_ref, T_R1, D_ROPE // 2)
    bckv_ref[:, 0:LANES] = lat
    bckv_ref[:, LANES:2 * LANES] = kr
    kb_ref[:, 0:LANES] = _mx(lat)
    kb_ref[:, LANES:2 * LANES] = _mx(kr)

    for i in range(C_HEADS // 2):
        z = proj(C_CQ + LANES * i, C_CQ + LANES * (i + 1)) * qscale
        h0, h1 = _split_heads(z)
        cq_ref[2 * i] = h0.astype(cq_ref.dtype)
        cq_ref[2 * i + 1] = h1.astype(cq_ref.dtype)
        h0, h1 = _split_heads(_rope(z, tab_ref, T_HH, HD // 2))
        cqr_ref[2 * i] = h0.astype(cqr_ref.dtype)
        cqr_ref[2 * i + 1] = h1.astype(cqr_ref.dtype)

    ccmp_ref[...] = proj(C_CCMP, C_CSLC)
    cslc = _rope(proj(C_CSLC, C_CWIN), tab_ref, T_HV, HD // 2)
    cslc_ref[...] = cslc
    cslcb_ref[...] = _mx(cslc)
    cwin = _rope(proj(C_CWIN, C_END), tab_ref, T_HV, HD // 2)
    cwin_ref[...] = cwin
    cwinb_ref[...] = _mx(cwin)


def _inproj(x, g, w, tab, qn, wuq, wuk, kvn, qdt):
    n, d = x.shape
    tm = min(1024, n)
    nt = tab.shape[0] // tm
    assert n % tm == 0 and tab.shape[0] % tm == 0
    row = lambda i: (i, 0)
    hrow = lambda i: (0, i, 0)
    const2 = lambda i: (0, 0)
    const3 = lambda i: (0, 0, 0)
    sds = jax.ShapeDtypeStruct
    out_shape = (
        sds((A_HEADS, n, LANES), qdt), sds((n, LANES), F32), sds((n, LANES), MXU_DTYPE),
        sds((IDX_HEADS, n, LANES), qdt), sds((n, LANES), F32), sds((n, LANES), MXU_DTYPE),
        sds((B_HEADS, n, 2 * LANES), qdt), sds((n, 2 * LANES), F32), sds((n, 2 * LANES), MXU_DTYPE),
        sds((C_HEADS, n, LANES), qdt), sds((C_HEADS, n, LANES), qdt),
        sds((n, LANES), F32), sds((n, LANES), F32), sds((n, LANES), MXU_DTYPE),
        sds((n, LANES), F32), sds((n, LANES), MXU_DTYPE))
    out_specs = (
        pl.BlockSpec((A_HEADS, tm, LANES), hrow), pl.BlockSpec((tm, LANES), row), pl.BlockSpec((tm, LANES), row),
        pl.BlockSpec((IDX_HEADS, tm, LANES), hrow), pl.BlockSpec((tm, LANES), row), pl.BlockSpec((tm, LANES), row),
        pl.BlockSpec((B_HEADS, tm, 2 * LANES), hrow), pl.BlockSpec((tm, 2 * LANES), row),
        pl.BlockSpec((tm, 2 * LANES), row),
        pl.BlockSpec((C_HEADS, tm, LANES), hrow), pl.BlockSpec((C_HEADS, tm, LANES), hrow),
        pl.BlockSpec((tm, LANES), row), pl.BlockSpec((tm, LANES), row), pl.BlockSpec((tm, LANES), row),
        pl.BlockSpec((tm, LANES), row), pl.BlockSpec((tm, LANES), row))
    return pl.pallas_call(
        _inproj_kernel,
        out_shape=out_shape,
        grid=(n // tm,),
        in_specs=[pl.BlockSpec((tm, d), row), pl.BlockSpec((1, d), const2),
                  pl.BlockSpec(w.shape, const2),
                  pl.BlockSpec((tm, tab.shape[1]), lambda i: (i % nt, 0)),
                  pl.BlockSpec((1, Q_LORA), const2), pl.BlockSpec(wuq.shape, const2),
                  pl.BlockSpec(wuk.shape, const3), pl.BlockSpec((1, KV_LORA), const2)],
        out_specs=out_specs,
        compiler_params=_cparams(1),
        name="inproj",
    )(x, g.reshape(1, d), w, tab, qn.reshape(1, Q_LORA), wuq, wuk, kvn.reshape(1, KV_LORA))


def _rope_table(pos):
    pos = pos.astype(F32)[:, None]

    def cs(half):
        inv = ROPE_THETA ** (-jnp.arange(half, dtype=F32) / half)
        ang = pos * inv[None, :]
        return jnp.cos(ang), jnp.sin(ang)

    n = pos.shape[0]
    c32, s32 = cs(HD // 2)
    c16, s16 = cs(D_ROPE // 2)
    one = lambda w: jnp.ones((n, w), F32)
    zero = lambda w: jnp.zeros((n, w), F32)
    cols = [jnp.concatenate([c32, c32, c32, c32], 1), jnp.concatenate([-s32, s32, -s32, s32], 1),
            jnp.concatenate([c32, c32, one(HD)], 1), jnp.concatenate([-s32, s32, zero(HD)], 1),
            jnp.concatenate([c16, c16, one(LANES - D_ROPE)], 1),
            jnp.concatenate([-s16, s16, zero(LANES - D_ROPE)], 1)]
    return jnp.concatenate(cols, axis=1)


def _layout_w_in(w):
    sizes = (A_HEADS * HD, HD, HD, IDX_HEADS * IDX_DIM, IDX_HEADS, IDX_DIM, Q_LORA, KV_LORA, D_ROPE,
             C_HEADS * HD, HD, HD, HD, HD, HD, HD, C_HEADS * N_GATES)
    offs = np.concatenate([[0], np.cumsum(sizes)])
    (a_q, a_k, a_v, a_iq, a_iw, a_ik, b_cq, b_ckv, b_kr,
     c_q, c_kc, c_vc, c_ks, c_vs, c_kw, c_vw, c_g) = [w[:, offs[i]:offs[i + 1]] for i in range(len(sizes))]
    z = lambda k: jnp.zeros((w.shape[0], k), w.dtype)
    misc_pad = LANES - IDX_DIM - IDX_HEADS - C_HEADS * N_GATES
    out = jnp.concatenate([a_q, a_k, a_v, a_iq, a_ik, a_iw, c_g, z(misc_pad), b_cq, b_ckv,
                           b_kr, z(LANES - D_ROPE), c_q, c_kc, c_vc, c_ks, c_vs, c_kw, c_vw], axis=1)
    assert out.shape[1] == C_END
    return _mx(out)


def _layout_mla(w_uq, w_uk, w_uv):
    d_qk = D_NOPE + D_ROPE
    uq = w_uq.reshape(Q_LORA, B_HEADS, d_qk)
    nope = uq[:, :, :D_NOPE].reshape(Q_LORA, B_HEADS * D_NOPE)
    rope = jnp.pad(uq[:, :, D_NOPE:], ((0, 0), (0, 0), (0, LANES - D_ROPE))).reshape(Q_LORA, B_HEADS * LANES)
    wuq = _mx(jnp.concatenate([nope, rope], axis=1))
    ukt = jnp.transpose(w_uk, (1, 2, 0))
    uvh = jnp.transpose(w_uv, (1, 0, 2))
    zk = jnp.zeros((D_NOPE, KV_LORA), w_uk.dtype)
    zv = jnp.zeros((KV_LORA, D_V), w_uv.dtype)
    wuk = jnp.stack([jnp.block([[ukt[2 * p], zk], [zk, ukt[2 * p + 1]]]) for p in range(B_HEADS // 2)])
    wuv = jnp.stack([jnp.block([[uvh[2 * p], zv], [zv, uvh[2 * p + 1]]]) for p in range(B_HEADS // 2)])
    return wuq, _mx(wuk), _mx(wuv)


def _dsa_prompt_kernel(n_keep, qi_ref, misc_ref, aq_ref, kidx_ref, akv_ref, akvt_ref, o_ref,
                       keys_ref, m_ref, l_ref, acc_ref):
    i = pl.program_id(1)
    tq = QBLOCK
    kc = keys_ref.shape[1]
    nch = (i * tq) // kc + 1
    qi = qi_ref[...].reshape(IDX_HEADS * tq, LANES)
    misc_t = misc_ref[...].T
    kpos0 = _row((kc, tq))
    qpos = i * tq + _lane((kc, tq))

    def score_chunk(c, carry):
        k = kidx_ref[pl.ds(pl.multiple_of(c * kc, kc), kc), :]
        rel = jnp.maximum(_dot_nt(k, qi), 0.0)
        score = jnp.zeros((kc, tq), F32)
        for hh in range(IDX_HEADS):
            score = score + rel[:, hh * tq:(hh + 1) * tq] * misc_t[MISC_W + hh:MISC_W + hh + 1, :]
        keys_ref[c] = jnp.where((c * kc + kpos0) <= qpos, _order_key(score), INT_MIN)
        return carry

    lax.fori_loop(0, nch, score_chunk, 0)

    def count_where(pred_fn):
        def body(c, cnt):
            hit = jnp.where(pred_fn(c, keys_ref[c]), 1, 0)
            return cnt + jnp.sum(hit.reshape(kc // 8, 8, tq), axis=0)
        cnt = lax.fori_loop(0, nch, body, jnp.zeros((8, tq), I32))
        return jnp.sum(cnt, axis=0, keepdims=True)

    def count_ge(cand):
        return count_where(lambda c, kk: kk >= cand)

    thr = jnp.maximum(_kth_largest(count_ge, n_keep, (1, tq)), INT_MIN + 1)

    @pl.when(jnp.max(count_ge(thr)) > n_keep)
    def _():
        need = n_keep - count_ge(thr + 1)
        cut = _tie_cut(lambda cand: count_where(lambda c, kk: (kk == thr) & ((c * kc + kpos0) < cand)),
                       need, int(kidx_ref.shape[0]).bit_length(), (1, tq))

        def drop(c, carry):
            kk = keys_ref[c]
            keys_ref[c] = jnp.where((kk == thr) & ((c * kc + kpos0) >= cut), INT_MIN, kk)
            return carry

        lax.fori_loop(0, nch, drop, 0)

    m_ref[...] = jnp.full_like(m_ref, NEG)
    l_ref[...] = jnp.zeros_like(l_ref)
    acc_ref[...] = jnp.zeros_like(acc_ref)
    qa = aq_ref[...].reshape(A_HEADS * tq, LANES)

    def attn_chunk(c, carry):
        kv = akv_ref[pl.ds(pl.multiple_of(c * kc, kc), kc), :]
        mask = jnp.concatenate([keys_ref[c] >= thr] * A_HEADS, axis=1)
        _flash_chunk_t(qa, kv, akvt_ref[c], mask, m_ref, l_ref, acc_ref)
        return carry

    lax.fori_loop(0, nch, attn_chunk, 0)
    ot = acc_ref[...] / l_ref[...]
    for hh in range(A_HEADS):
        o_ref[:, hh * HD:(hh + 1) * HD] = ot[:, hh * tq:(hh + 1) * tq].T[:, HD:].astype(o_ref.dtype)


def _dsa_prompt(aiq, misc, aq, aikb, akvb, b, t):
    n = b * t
    nq = t // QBLOCK
    kc = min(KCHUNK, t)
    n_keep = min(DSA_TOPK, t // 4)
    qmap = lambda bb, i: (0, bb * nq + i, 0)
    akvt = _chunked_transpose(akvb, b, t, kc)
    return pl.pallas_call(
        functools.partial(_dsa_prompt_kernel, n_keep),
        out_shape=jax.ShapeDtypeStruct((n, A_HEADS * HD), MXU_DTYPE),
        grid=(b, nq),
        in_specs=[pl.BlockSpec((IDX_HEADS, QBLOCK, LANES), qmap),
                  pl.BlockSpec((QBLOCK, LANES), lambda bb, i: (bb * nq + i, 0)),
                  pl.BlockSpec((A_HEADS, QBLOCK, LANES), qmap),
                  pl.BlockSpec((t, LANES), lambda bb, i: (bb, 0)),
                  pl.BlockSpec((t, LANES), lambda bb, i: (bb, 0)),
                  pl.BlockSpec((None, t // kc, LANES, kc), lambda bb, i: (bb, 0, 0, 0))],
        out_specs=pl.BlockSpec((QBLOCK, A_HEADS * HD), lambda bb, i: (bb * nq + i, 0)),
        scratch_shapes=[pltpu.VMEM((t // kc, kc, QBLOCK), I32),
                        pltpu.VMEM((1, A_HEADS * QBLOCK), F32), pltpu.VMEM((1, A_HEADS * QBLOCK), F32),
                        pltpu.VMEM((LANES, A_HEADS * QBLOCK), F32)],
        compiler_params=_cparams(2),
        name="dsa_prompt",
    )(aiq, misc, aq, aikb, akvb, akvt)


def _mla_prompt_kernel(q_ref, kb_ref, latt_ref, wuv_ref, o_ref, m_ref, l_ref, acc_ref):
    i = pl.program_id(1)
    tq = QBLOCK
    kc = latt_ref.shape[2]
    rows = B_HEADS * tq
    q = q_ref[...].reshape(rows, 2 * LANES)
    m_ref[...] = jnp.full_like(m_ref, NEG)
    l_ref[...] = jnp.zeros_like(l_ref)
    acc_ref[...] = jnp.zeros_like(acc_ref)

    def chunk(c, masked):
        k = kb_ref[pl.ds(pl.multiple_of(c * kc, kc), kc), :]
        mask = None
        if masked:
            mask = (c * kc + _row((kc, rows))) <= (i * tq + _lane((kc, rows)) % tq)
        _flash_chunk_t(q, k, latt_ref[c], mask, m_ref, l_ref, acc_ref)

    nfull = (i * tq) // kc

    def body(c, carry):
        chunk(c, False)
        return carry

    lax.fori_loop(0, nfull, body, 0)
    chunk(nfull, True)
    ot = acc_ref[...] / l_ref[...]
    for p in range(B_HEADS // 2):
        pair = jnp.concatenate([ot[:, 2 * p * tq:(2 * p + 1) * tq].T, ot[:, (2 * p + 1) * tq:(2 * p + 2) * tq].T],
                               axis=1)
        o_ref[:, p * LANES:(p + 1) * LANES] = _dot(_mx(pair), wuv_ref[p]).astype(o_ref.dtype)


def _mla_prompt(qcat, kb, wuv, b, t):
    n = b * t
    nq = t // QBLOCK
    kc = min(KCHUNK, t)
    latt = _chunked_transpose(kb[:, :KV_LORA], b, t, kc)
    return pl.pallas_call(
        _mla_prompt_kernel,
        out_shape=jax.ShapeDtypeStruct((n, B_HEADS * D_V), MXU_DTYPE),
        grid=(b, nq),
        in_specs=[pl.BlockSpec((B_HEADS, QBLOCK, 2 * LANES), lambda bb, i: (0, bb * nq + i, 0)),
                  pl.BlockSpec((t, 2 * LANES), lambda bb, i: (bb, 0)),
                  pl.BlockSpec((None, t // kc, KV_LORA, kc), lambda bb, i: (bb, 0, 0, 0)),
                  pl.BlockSpec(wuv.shape, lambda bb, i: (0, 0, 0))],
        out_specs=pl.BlockSpec((QBLOCK, B_HEADS * D_V), lambda bb, i: (bb * nq + i, 0)),
        scratch_shapes=[pltpu.VMEM((1, B_HEADS * QBLOCK), F32), pltpu.VMEM((1, B_HEADS * QBLOCK), F32),
                        pltpu.VMEM((KV_LORA, B_HEADS * QBLOCK), F32)],
        compiler_params=_cparams(2),
        name="mla_prompt",
    )(qcat, kb, latt, wuv)


def _chunk_sums(x, wp):
    x3 = x.reshape(x.shape[0] // CMP_STRIDE, CMP_STRIDE, LANES)
    p0 = jnp.sum(x3 * wp[None, :CMP_STRIDE], axis=1)
    p1 = jnp.sum(x3 * wp[None, CMP_STRIDE:], axis=1)
    return jnp.concatenate([p0, p1], axis=1)


def _cmp_rows_kernel(x_ref, wp_ref, p_ref):
    p_ref[...] = _chunk_sums(x_ref[...], wp_ref[...])


def _cmp_rows(rows, wp, b, t):
    tr = min(1024, t)
    return pl.pallas_call(
        _cmp_rows_kernel,
        out_shape=jax.ShapeDtypeStruct((b * t // CMP_STRIDE, 2 * LANES), F32),
        grid=(b * t // tr,),
        in_specs=[pl.BlockSpec((tr, LANES), lambda i: (i, 0)), pl.BlockSpec(wp.shape, lambda i: (0, 0))],
        out_specs=pl.BlockSpec((tr // CMP_STRIDE, 2 * LANES), lambda i: (i, 0)),
        compiler_params=_cparams(1),
        name="cmp_rows",
    )(rows, wp)


def _cmp_pages_kernel(npg, pt_ref, *refs):
    pages, wp_ref, p_ref = refs[:npg], refs[npg], refs[npg + 1]
    cpp = PAGE // CMP_STRIDE
    for g in range(npg):
        p_ref[g * cpp:(g + 1) * cpp, :] = _chunk_sums(pages[g][...], wp_ref[...])


def _page_specs(npg, width, layer, n_pages, transposed=False):
    block = (None, None, width, PAGE) if transposed else (None, None, PAGE, width)

    def spec(g):
        return pl.BlockSpec(block, lambda bb, s, pt: (layer, pt[bb * n_pages + s * npg + g], 0, 0))
    return [spec(g) for g in range(npg)]


def _cmp_pages(pool, layer, pt, wp, b, n_pages):
    npg = min(PAGES_PER_STEP, n_pages)
    cpp = PAGE // CMP_STRIDE
    grid_spec = pltpu.PrefetchScalarGridSpec(
        num_scalar_prefetch=1, grid=(b, n_pages // npg),
        in_specs=_page_specs(npg, LANES, layer, n_pages) + [pl.BlockSpec(wp.shape, lambda bb, s, pt: (0, 0))],
        out_specs=pl.BlockSpec((None, npg * cpp, 2 * LANES), lambda bb, s, pt: (bb, s, 0)))
    return pl.pallas_call(
        functools.partial(_cmp_pages_kernel, npg),
        out_shape=jax.ShapeDtypeStruct((b, n_pages * cpp, 2 * LANES), F32),
        grid_spec=grid_spec,
        compiler_params=_cparams(2),
        name="cmp_pages",
    )(pt, *([pool] * npg), wp)


def _cmp_finish_kernel(n_cmp, p0_ref, p1_ref, w_ref, o_ref):
    blk = p0_ref[:, :LANES] + p1_ref[...]
    out = _dot(_mx(blk), w_ref[...])
    o_ref[...] = jnp.where(_row(out.shape) < n_cmp, out, 0.0).astype(o_ref.dtype)


def _cmp_finish(p, wproj, n_cmp):
    b, n_chunk, _ = p.shape
    p1s = jnp.pad(p[:, 1:, LANES:], ((0, 0), (0, 1), (0, 0)))
    return pl.pallas_call(
        functools.partial(_cmp_finish_kernel, n_cmp),
        out_shape=jax.ShapeDtypeStruct((b, n_chunk, LANES), MXU_DTYPE),
        grid=(b,),
        in_specs=[pl.BlockSpec((None, n_chunk, 2 * LANES), lambda bb: (bb, 0, 0)),
                  pl.BlockSpec((None, n_chunk, LANES), lambda bb: (bb, 0, 0)),
                  pl.BlockSpec(wproj.shape, lambda bb: (0, 0))],
        out_specs=pl.BlockSpec((None, n_chunk, LANES), lambda bb: (bb, 0, 0)),
        compiler_params=_cparams(1),
        name="cmp_finish",
    )(p, p1s, wproj)


def _band_matrix(n_rows, n_cols):
    r = SLC_BLOCK // CMP_STRIDE
    c = CMP_LEN // CMP_STRIDE
    i = np.arange(n_rows)[:, None]
    j = np.arange(n_cols)[None, :]
    return jnp.asarray((i >= r * j - (c - 1)) & (i <= r * j + r - 1), MXU_DTYPE)


def _importance(pc, band):
    hi = pc.astype(jnp.bfloat16)
    r1 = pc - hi.astype(F32)
    mid = r1.astype(jnp.bfloat16)
    lo = (r1 - mid.astype(F32)).astype(jnp.bfloat16)
    band = band.astype(jnp.bfloat16)
    return _dot(hi, band) + _dot(mid, band) + _dot(lo, band)


def _select_blocks(imp, qpos, n_slc, n_top):
    j = _lane(imp.shape)
    cur = qpos // SLC_BLOCK
    forced = (j == 0) | (j == cur) | (j == cur - 1)
    adm = (j * SLC_BLOCK <= qpos) & (j < n_slc)
    score = jnp.where(forced, FORCE_SCORE, jnp.where(adm, imp, -jnp.inf))
    key = jnp.where(adm, _order_key(score), INT_MIN)
    return jnp.where(_topk_mask(key, j, n_top, int(imp.shape[1]).bit_length()), 1.0, 0.0)


def _masked_softmax(s, mask):
    s = jnp.where(mask, s, NEG)
    m = jnp.max(s, axis=-1, keepdims=True)
    e = jnp.where(mask, jnp.exp(s - m), 0.0)
    return e / jnp.maximum(jnp.sum(e, axis=-1, keepdims=True), jnp.finfo(F32).tiny)


def _nsa_prompt_kernel(n_cmp, n_slc, n_top, cq_ref, cqr_ref, misc_ref, kvc_ref, band_ref,
                       slc_ref, slct_ref, win_ref, o_ref, m_ref, l_ref, acc_ref):
    i = pl.program_id(1)
    tq = QBLOCK
    t = slc_ref.shape[0]
    kc = min(KCHUNK, t)
    q = cq_ref[...].reshape(C_HEADS * tq, LANES)
    qr = cqr_ref[...].reshape(C_HEADS * tq, LANES)

    kvc = kvc_ref[...]
    ncp = kvc.shape[0]
    qpos_c = i * tq + _row((tq, ncp))
    n_id = _lane((tq, ncp))
    cmask = (n_id * CMP_STRIDE + CMP_LEN - 1 <= qpos_c) & (n_id < n_cmp)
    p_cmp = _masked_softmax(_dot_nt(q, kvc), _tile_rows(cmask, C_HEADS))
    o_cmp = _dot(_mx(p_cmp), kvc)
    pc = p_cmp[0:tq]
    for hh in range(1, C_HEADS):
        pc = pc + p_cmp[hh * tq:(hh + 1) * tq]

    imp = _importance(pc, band_ref[...])
    sel = _select_blocks(imp, i * tq + _row(imp.shape), n_slc, n_top)

    m_ref[...] = jnp.full_like(m_ref, NEG)
    l_ref[...] = jnp.zeros_like(l_ref)
    acc_ref[...] = jnp.zeros_like(acc_ref)
    nsl = sel.shape[1]
    sel_t = sel.T.astype(jnp.bfloat16)
    qpos = i * tq + _lane((kc, tq))

    def slc_chunk(c, carry):
        kv = slc_ref[pl.ds(pl.multiple_of(c * kc, kc), kc), :]
        kblock = (c * kc + _row((kc, nsl))) // SLC_BLOCK
        expand = jnp.where(_lane((kc, nsl)) == kblock, 1.0, 0.0).astype(jnp.bfloat16)
        visible = (c * kc + _row((kc, tq))) <= qpos
        picked = jnp.where(visible, _dot(expand, sel_t), 0.0) > 0.5
        _flash_chunk_t(qr, kv, slct_ref[c], jnp.concatenate([picked] * C_HEADS, axis=1), m_ref, l_ref, acc_ref)
        return carry

    lax.fori_loop(0, (i * tq) // kc + 1, slc_chunk, 0)
    ot = acc_ref[...] / jnp.maximum(l_ref[...], jnp.finfo(F32).tiny)
    o_slc = jnp.concatenate([ot[:, hh * tq:(hh + 1) * tq].T for hh in range(C_HEADS)], axis=0)

    nw = min(WINDOW + tq, t)
    start = jnp.maximum(i * tq + tq - nw, 0)
    rows = win_ref[pl.ds(pl.multiple_of(start, tq), nw), :]
    dist = (i * tq + _row((tq, nw))) - (start + _lane((tq, nw)))
    wmask = _tile_rows((dist >= 0) & (dist < WINDOW), C_HEADS)
    o_win = _dot(_mx(_masked_softmax(_dot_nt(qr, rows), wmask)), rows)

    misc = misc_ref[...]
    for hh in range(C_HEADS):
        g = [misc[:, MISC_G + N_GATES * hh + k:MISC_G + N_GATES * hh + k + 1] for k in range(N_GATES)]
        sl = slice(hh * tq, (hh + 1) * tq)
        o = g[0] * o_cmp[sl] + g[1] * o_slc[sl] + g[2] * o_win[sl]
        o_ref[:, hh * HD:(hh + 1) * HD] = o[:, HD:].astype(o_ref.dtype)


def _nsa_prompt(cq, cqr, misc, kvc, slcb, winb, b, t):
    n = b * t
    nq = t // QBLOCK
    n_chunk = t // CMP_STRIDE
    n_cmp = n_chunk - CMP_LEN // CMP_STRIDE + 1
    n_slc = -(-t // SLC_BLOCK)
    n_top = min(N_SLC, n_slc)
    nsl = -(-n_slc // LANES) * LANES
    band = _band_matrix(n_chunk, nsl)
    kc = min(KCHUNK, t)
    qmap = lambda bb, i: (0, bb * nq + i, 0)
    return pl.pallas_call(
        functools.partial(_nsa_prompt_kernel, n_cmp, n_slc, n_top),
        out_shape=jax.ShapeDtypeStruct((n, C_HEADS * HD), MXU_DTYPE),
        grid=(b, nq),
        in_specs=[pl.BlockSpec((C_HEADS, QBLOCK, LANES), qmap),
                  pl.BlockSpec((C_HEADS, QBLOCK, LANES), qmap),
                  pl.BlockSpec((QBLOCK, LANES), lambda bb, i: (bb * nq + i, 0)),
                  pl.BlockSpec((None, n_chunk, LANES), lambda bb, i: (bb, 0, 0)),
                  pl.BlockSpec(band.shape, lambda bb, i: (0, 0)),
                  pl.BlockSpec((t, LANES), lambda bb, i: (bb, 0)),
                  pl.BlockSpec((None, t // kc, LANES, kc), lambda bb, i: (bb, 0, 0, 0)),
                  pl.BlockSpec((t, LANES), lambda bb, i: (bb, 0))],
        out_specs=pl.BlockSpec((QBLOCK, C_HEADS * HD), lambda bb, i: (bb * nq + i, 0)),
        scratch_shapes=[pltpu.VMEM((1, C_HEADS * QBLOCK), F32), pltpu.VMEM((1, C_HEADS * QBLOCK), F32),
                        pltpu.VMEM((LANES, C_HEADS * QBLOCK), F32)],
        compiler_params=_cparams(2),
        name="nsa_prompt",
    )(cq, cqr, misc, kvc, band, slcb, _chunked_transpose(slcb, b, t, kc), winb)


def _stack_heads(ref):
    x = ref[...]
    return _mx(x.reshape(x.shape[0] * x.shape[1], x.shape[2]))


def _new_mask(shape, dec):
    return (_lane(shape) <= _row(shape) % dec) & (_lane(shape) < dec)


def _dsa_scores_kernel(npg, pt_ref, *refs):
    qi_ref, misc_ref = refs[0], refs[1]
    pages = refs[2:2 + npg]
    newk_ref, scp_ref, scn_ref, kbuf = refs[2 + npg:6 + npg]
    dec = misc_ref.shape[0]
    q = _stack_heads(qi_ref)[:, :IDX_DIM]
    misc = misc_ref[...]

    def weighted(rel):
        score = jnp.zeros((dec, rel.shape[1]), F32)
        for hh in range(IDX_HEADS):
            score = score + rel[hh * dec:(hh + 1) * dec] * misc[:, MISC_W + hh:MISC_W + hh + 1]
        return score

    for g in range(npg):
        kbuf[:, g * PAGE:(g + 1) * PAGE] = _mx(pages[g][...])
    scp_ref[...] = weighted(jnp.maximum(_dot(q, kbuf[...]), 0.0))

    @pl.when(pl.program_id(1) == 0)
    def _():
        sc = weighted(jnp.maximum(_dot_nt(q, newk_ref[:, :IDX_DIM]), 0.0))
        scn_ref[...] = jnp.where(_new_mask(sc.shape, dec), sc, -jnp.inf)


def _dsa_scores(aiq, misc, pool, layer, pt, newk, b, dec, n_pages):
    npg = min(PAGES_PER_STEP, n_pages)
    grid_spec = pltpu.PrefetchScalarGridSpec(
        num_scalar_prefetch=1, grid=(b, n_pages // npg),
        in_specs=[pl.BlockSpec((IDX_HEADS, dec, LANES), lambda bb, s, pt: (0, bb, 0)),
                  pl.BlockSpec((dec, LANES), lambda bb, s, pt: (bb, 0))]
        + _page_specs(npg, IDX_DIM, layer, n_pages, transposed=True)
        + [pl.BlockSpec((None, PAGE, LANES), lambda bb, s, pt: (bb, 0, 0))],
        out_specs=(pl.BlockSpec((None, dec, npg * PAGE), lambda bb, s, pt: (bb, 0, s)),
                   pl.BlockSpec((None, dec, LANES), lambda bb, s, pt: (bb, 0, 0))),
        scratch_shapes=[pltpu.VMEM((IDX_DIM, npg * PAGE), MXU_DTYPE)])
    return pl.pallas_call(
        functools.partial(_dsa_scores_kernel, npg),
        out_shape=(jax.ShapeDtypeStruct((b, dec, n_pages * PAGE), F32),
                   jax.ShapeDtypeStruct((b, dec, LANES), F32)),
        grid_spec=grid_spec,
        compiler_params=_cparams(2),
        name="dsa_scores",
    )(pt, aiq, misc, *([pool] * npg), newk)


def _dsa_sample_kernel(npg, n_keep, pt_ref, *refs):
    aq_ref, scfull_ref, scn_ref, scstep_ref = refs[:4]
    pages = refs[4:4 + npg]
    newkv_ref, o_ref, thr_ref, cut_ref, m_ref, l_ref, acc_ref, kvbuf = refs[4 + npg:12 + npg]
    dec = scn_ref.shape[0]
    past_len = scfull_ref.shape[1]
    s_id = pl.program_id(1)
    q = _stack_heads(aq_ref)

    def picked(key, pos):
        thr = thr_ref[:, 0:1]
        return (key > thr) | ((key == thr) & (pos < cut_ref[:, 0:1]))

    @pl.when(s_id == 0)
    def _():
        new_ok = _new_mask((dec, LANES), dec)
        key_new = jnp.where(new_ok, _order_key(scn_ref[...]), INT_MIN)
        key_all = _order_key(scfull_ref[...])
        pos_all = _lane(key_all.shape)
        pos_new = past_len + _lane(key_new.shape)

        def count(pred_all, pred_new):
            return (jnp.sum(jnp.where(pred_all, 1, 0), axis=-1, keepdims=True)
                    + jnp.sum(jnp.where(pred_new, 1, 0), axis=-1, keepdims=True))

        thr = jnp.maximum(_kth_largest(lambda c: count(key_all >= c, key_new >= c), n_keep, (dec, 1)),
                          INT_MIN + 1)
        need = n_keep - count(key_all > thr, key_new > thr)
        cut = _tie_cut(lambda c: count((key_all == thr) & (pos_all < c), (key_new == thr) & (pos_new < c)),
                       need, int(past_len + LANES).bit_length(), (dec, 1))
        thr_ref[...] = jnp.broadcast_to(thr, thr_ref.shape)
        cut_ref[...] = jnp.broadcast_to(cut, cut_ref.shape)
        m_ref[...] = jnp.full_like(m_ref, NEG)
        l_ref[...] = jnp.zeros_like(l_ref)
        acc_ref[...] = jnp.zeros_like(acc_ref)
        kv = newkv_ref[...]
        mask = _tile_rows(picked(key_new, pos_new), A_HEADS)
        _softmax_update(jnp.where(mask, _dot_nt(q, kv), NEG), mask, m_ref, l_ref, acc_ref, kv)

    for g in range(npg):
        kvbuf[g * PAGE:(g + 1) * PAGE, :] = _mx(pages[g][...])
    kv = kvbuf[...]
    key_step = _order_key(scstep_ref[...])
    mask = _tile_rows(picked(key_step, s_id * (npg * PAGE) + _lane(key_step.shape)), A_HEADS)
    _softmax_update(jnp.where(mask, _dot_nt(q, kv), NEG), mask, m_ref, l_ref, acc_ref, kv)

    @pl.when(s_id == pl.num_programs(1) - 1)
    def _():
        o = acc_ref[...] / l_ref[...]
        for hh in range(A_HEADS):
            o_ref[:, hh * HD:(hh + 1) * HD] = o[hh * dec:(hh + 1) * dec, HD:]


def _dsa_sample(aq, scp, scn, pool, layer, pt, newkv, b, dec, n_pages):
    npg = min(PAGES_PER_STEP, n_pages)
    n_keep = min(DSA_TOPK, (n_pages * PAGE + dec) // 4)
    grid_spec = pltpu.PrefetchScalarGridSpec(
        num_scalar_prefetch=1, grid=(b, n_pages // npg),
        in_specs=[pl.BlockSpec((A_HEADS, dec, LANES), lambda bb, s, pt: (0, bb, 0)),
                  pl.BlockSpec((None, dec, n_pages * PAGE), lambda bb, s, pt: (bb, 0, 0)),
                  pl.BlockSpec((None, dec, LANES), lambda bb, s, pt: (bb, 0, 0)),
                  pl.BlockSpec((None, dec, npg * PAGE), lambda bb, s, pt: (bb, 0, s))]
        + _page_specs(npg, 2 * HD, layer, n_pages)
        + [pl.BlockSpec((None, PAGE, LANES), lambda bb, s, pt: (bb, 0, 0))],
        out_specs=pl.BlockSpec((dec, A_HEADS * HD), lambda bb, s, pt: (bb, 0)),
        scratch_shapes=[pltpu.VMEM((dec, LANES), I32), pltpu.VMEM((dec, LANES), I32),
                        pltpu.VMEM((A_HEADS * dec, 1), F32), pltpu.VMEM((A_HEADS * dec, 1), F32),
                        pltpu.VMEM((A_HEADS * dec, LANES), F32),
                        pltpu.VMEM((npg * PAGE, LANES), MXU_DTYPE)])
    return pl.pallas_call(
        functools.partial(_dsa_sample_kernel, npg, n_keep),
        out_shape=jax.ShapeDtypeStruct((b * dec, A_HEADS * HD), F32),
        grid_spec=grid_spec,
        compiler_params=_cparams(2),
        name="dsa_sample",
    )(pt, aq, scp, scn, scp, *([pool] * npg), newkv)


def _mla_sample_kernel(npg, pt_ref, *refs):
    q_ref = refs[0]
    pages = refs[1:1 + npg]
    newkb_ref, wuv_ref, o_ref, m_ref, l_ref, acc_ref, kbuf = refs[1 + npg:8 + npg]
    dec = o_ref.shape[0]
    s_id = pl.program_id(1)
    q = _stack_heads(q_ref)
    wk = KV_LORA + D_ROPE

    @pl.when(s_id == 0)
    def _():
        kbuf[...] = jnp.zeros_like(kbuf)
        m_ref[...] = jnp.full_like(m_ref, NEG)
        l_ref[...] = jnp.zeros_like(l_ref)
        acc_ref[...] = jnp.zeros_like(acc_ref)
        k = newkb_ref[...]
        mask = _new_mask((B_HEADS * dec, PAGE), dec)
        _softmax_update(jnp.where(mask, _dot_nt(q, k), NEG), mask, m_ref, l_ref, acc_ref, k[:, :LANES])

    for g in range(npg):
        kbuf[0:wk, g * PAGE:(g + 1) * PAGE] = _mx(pages[g][...])
    _softmax_update(_dot(q, kbuf[...]), None, m_ref, l_ref, acc_ref, kbuf[0:KV_LORA, :], v_transposed=True)

    @pl.when(s_id == pl.num_programs(1) - 1)
    def _():
        o = _mx(acc_ref[...] / l_ref[...])
        for p in range(B_HEADS // 2):
            pair = jnp.concatenate([o[2 * p * dec:(2 * p + 1) * dec], o[(2 * p + 1) * dec:(2 * p + 2) * dec]], axis=1)
            o_ref[:, p * LANES:(p + 1) * LANES] = _dot(pair, wuv_ref[p])


def _mla_sample(qcat, pool, layer, pt, newkb, wuv, b, dec, n_pages):
    npg = min(PAGES_PER_STEP, n_pages)
    grid_spec = pltpu.PrefetchScalarGridSpec(
        num_scalar_prefetch=1, grid=(b, n_pages // npg),
        in_specs=[pl.BlockSpec((B_HEADS, dec, 2 * LANES), lambda bb, s, pt: (0, bb, 0))]
        + _page_specs(npg, KV_LORA + D_ROPE, layer, n_pages, transposed=True)
        + [pl.BlockSpec((None, PAGE, 2 * LANES), lambda bb, s, pt: (bb, 0, 0)),
           pl.BlockSpec(wuv.shape, lambda bb, s, pt: (0, 0, 0))],
        out_specs=pl.BlockSpec((dec, B_HEADS * D_V), lambda bb, s, pt: (bb, 0)),
        scratch_shapes=[pltpu.VMEM((B_HEADS * dec, 1), F32), pltpu.VMEM((B_HEADS * dec, 1), F32),
                        pltpu.VMEM((B_HEADS * dec, LANES), F32),
                        pltpu.VMEM((2 * LANES, npg * PAGE), MXU_DTYPE)])
    return pl.pallas_call(
        functools.partial(_mla_sample_kernel, npg),
        out_shape=jax.ShapeDtypeStruct((b * dec, B_HEADS * D_V), F32),
        grid_spec=grid_spec,
        compiler_params=_cparams(2),
        name="mla_sample",
    )(pt, qcat, *([pool] * npg), newkb, wuv)


def _nsa_select_kernel(n_cmp, n_slc, n_top, past_len, bps, cq_ref, cqr_ref, kvc_ref, band_ref,
                       state_ref, neww_ref, newwb_ref, ocw_ref, sel_ref, wstate_ref):
    dec = neww_ref.shape[0]
    q = _stack_heads(cq_ref)
    qr = _stack_heads(cqr_ref)
    rows_q = C_HEADS * dec

    kvc = kvc_ref[...]
    ncp = kvc.shape[0]
    qpos_c = past_len + _row((dec, ncp)) % dec
    n_id = _lane((dec, ncp))
    cmask = (n_id * CMP_STRIDE + CMP_LEN - 1 <= qpos_c) & (n_id < n_cmp)
    p_cmp = _masked_softmax(_dot_nt(q, kvc), _tile_rows(cmask, C_HEADS))
    ocw_ref[0] = _dot(_mx(p_cmp), kvc)
    pc = p_cmp[0:dec]
    for hh in range(1, C_HEADS):
        pc = pc + p_cmp[hh * dec:(hh + 1) * dec]

    imp = _importance(pc, band_ref[...])
    sel = _select_blocks(imp, past_len + _row(imp.shape), n_slc, n_top)
    for s in range(sel_ref.shape[0]):
        tile = sel[:, (s * bps // LANES) * LANES:(s * bps // LANES + 1) * LANES]
        shift = (s * bps) % LANES
        sel_ref[s] = pltpu.roll(tile, LANES - shift, 1) if shift else tile

    state = state_ref[...]
    nst = state.shape[0]
    sb = _mx(state)
    nb = newwb_ref[...]
    trow = _row((rows_q, nst)) % dec
    dist = nst + trow - _lane((rows_q, nst))
    m1 = (dist >= 0) & (dist < WINDOW)
    m2 = _new_mask((rows_q, PAGE), dec)
    s1 = jnp.where(m1, _dot_nt(qr, sb), NEG)
    s2 = jnp.where(m2, _dot_nt(qr, nb), NEG)
    m = jnp.maximum(jnp.max(s1, axis=-1, keepdims=True), jnp.max(s2, axis=-1, keepdims=True))
    e1 = jnp.where(m1, jnp.exp(s1 - m), 0.0)
    e2 = jnp.where(m2, jnp.exp(s2 - m), 0.0)
    den = jnp.maximum(jnp.sum(e1, axis=-1, keepdims=True) + jnp.sum(e2, axis=-1, keepdims=True),
                      jnp.finfo(F32).tiny)
    ocw_ref[1] = _dot(_mx(e1 / den), sb) + _dot(_mx(e2 / den), nb)

    wstate_ref[0:nst - dec, :] = state[dec:, :]
    wstate_ref[nst - dec:nst, :] = neww_ref[...]


def _nsa_select(cq, cqr, kvc, state, layer, neww, newwb, b, dec, n_pages, npg):
    past_len = n_pages * PAGE
    total = past_len + dec
    n_chunk = kvc.shape[1]
    n_cmp = total // CMP_STRIDE - CMP_LEN // CMP_STRIDE + 1
    n_slc = -(-total // SLC_BLOCK)
    n_top = min(N_SLC, n_slc)
    bps = npg * PAGE // SLC_BLOCK
    nsteps = n_pages // npg
    nsl = -(-max(n_slc, nsteps * bps) // LANES) * LANES
    band = _band_matrix(n_chunk, nsl)
    nst = state.shape[2]
    return pl.pallas_call(
        functools.partial(_nsa_select_kernel, n_cmp, n_slc, n_top, past_len, bps),
        out_shape=(jax.ShapeDtypeStruct((b, 2, C_HEADS * dec, LANES), F32),
                   jax.ShapeDtypeStruct((b, nsteps, dec, LANES), F32),
                   jax.ShapeDtypeStruct((b, nst, LANES), F32)),
        grid=(b,),
        in_specs=[pl.BlockSpec((C_HEADS, dec, LANES), lambda bb: (0, bb, 0)),
                  pl.BlockSpec((C_HEADS, dec, LANES), lambda bb: (0, bb, 0)),
                  pl.BlockSpec((None, n_chunk, LANES), lambda bb: (bb, 0, 0)),
                  pl.BlockSpec(band.shape, lambda bb: (0, 0)),
                  pl.BlockSpec((None, None, nst, LANES), lambda bb: (layer, bb, 0, 0)),
                  pl.BlockSpec((dec, LANES), lambda bb: (bb, 0)),
                  pl.BlockSpec((None, PAGE, LANES), lambda bb: (bb, 0, 0))],
        out_specs=(pl.BlockSpec((None, 2, C_HEADS * dec, LANES), lambda bb: (bb, 0, 0, 0)),
                   pl.BlockSpec((None, nsteps, dec, LANES), lambda bb: (bb, 0, 0, 0)),
                   pl.BlockSpec((None, nst, LANES), lambda bb: (bb, 0, 0))),
        compiler_params=_cparams(1),
        name="nsa_select",
    )(cq, cqr, kvc, band, state, neww, newwb)


def _nsa_slc_kernel(npg, pt_ref, *refs):
    cqr_ref, misc_ref, ocw_ref, sel_ref = refs[:4]
    pages = refs[4:4 + npg]
    newb_ref, o_ref, m_ref, l_ref, acc_ref, kvbuf = refs[4 + npg:10 + npg]
    dec = misc_ref.shape[0]
    s_id = pl.program_id(1)
    qr = _stack_heads(cqr_ref)

    @pl.when(s_id == 0)
    def _():
        m_ref[...] = jnp.full_like(m_ref, NEG)
        l_ref[...] = jnp.zeros_like(l_ref)
        acc_ref[...] = jnp.zeros_like(acc_ref)
        kv = newb_ref[...]
        mask = _new_mask((C_HEADS * dec, PAGE), dec)
        _softmax_update(jnp.where(mask, _dot_nt(qr, kv), NEG), mask, m_ref, l_ref, acc_ref, kv)

    sel = sel_ref[...]
    low = _lane((dec, PAGE)) < SLC_BLOCK
    masks = []
    for g in range(npg):
        kvbuf[g * PAGE:(g + 1) * PAGE, :] = _mx(pages[g][...])
        masks.append(jnp.where(low, sel[:, 2 * g:2 * g + 1], sel[:, 2 * g + 1:2 * g + 2]) > 0.5)
    mask = _tile_rows(jnp.concatenate(masks, axis=1), C_HEADS)
    kv = kvbuf[...]
    _softmax_update(jnp.where(mask, _dot_nt(qr, kv), NEG), mask, m_ref, l_ref, acc_ref, kv)

    @pl.when(s_id == pl.num_programs(1) - 1)
    def _():
        o_slc = acc_ref[...] / jnp.maximum(l_ref[...], jnp.finfo(F32).tiny)
        misc = misc_ref[...]
        o_cmp = ocw_ref[0]
        o_win = ocw_ref[1]
        for hh in range(C_HEADS):
            g = [misc[:, MISC_G + N_GATES * hh + k:MISC_G + N_GATES * hh + k + 1] for k in range(N_GATES)]
            sl = slice(hh * dec, (hh + 1) * dec)
            o = g[0] * o_cmp[sl] + g[1] * o_slc[sl] + g[2] * o_win[sl]
            o_ref[:, hh * HD:(hh + 1) * HD] = o[:, HD:]


def _nsa_slc(cqr, misc, ocw, selsteps, pool, layer, pt, newb, b, dec, n_pages, npg):
    grid_spec = pltpu.PrefetchScalarGridSpec(
        num_scalar_prefetch=1, grid=(b, n_pages // npg),
        in_specs=[pl.BlockSpec((C_HEADS, dec, LANES), lambda bb, s, pt: (0, bb, 0)),
                  pl.BlockSpec((dec, LANES), lambda bb, s, pt: (bb, 0)),
                  pl.BlockSpec((None, 2, C_HEADS * dec, LANES), lambda bb, s, pt: (bb, 0, 0, 0)),
                  pl.BlockSpec((None, None, dec, LANES), lambda bb, s, pt: (bb, s, 0, 0))]
        + _page_specs(npg, 2 * HD, layer, n_pages)
        + [pl.BlockSpec((None, PAGE, LANES), lambda bb, s, pt: (bb, 0, 0))],
        out_specs=pl.BlockSpec((dec, C_HEADS * HD), lambda bb, s, pt: (bb, 0)),
        scratch_shapes=[pltpu.VMEM((C_HEADS * dec, 1), F32), pltpu.VMEM((C_HEADS * dec, 1), F32),
                        pltpu.VMEM((C_HEADS * dec, LANES), F32),
                        pltpu.VMEM((npg * PAGE, LANES), MXU_DTYPE)])
    return pl.pallas_call(
        functools.partial(_nsa_slc_kernel, npg),
        out_shape=jax.ShapeDtypeStruct((b * dec, C_HEADS * HD), F32),
        grid_spec=grid_spec,
        compiler_params=_cparams(2),
        name="nsa_slc",
    )(pt, cqr, misc, ocw, selsteps, *([pool] * npg), newb)


def _new_pages(rows, b, dec):
    r = rows.reshape(b, dec, rows.shape[-1])
    return _mx(jnp.pad(r, ((0, 0), (0, PAGE - dec), (0, 0))))


def _layer_weights(l, p):
    wuq, wuk, wuv = _layout_mla(p['b_w_uq'][l], p['b_w_uk'][l], p['b_w_uv'][l])
    wout = _mx(p['w_out'][l])
    na, nb = A_HEADS * HD, B_HEADS * D_V
    zero = jnp.zeros((HD, HD), F32)
    return dict(
        w_in=_layout_w_in(p['w_in'][l]), wuq=wuq, wuk=wuk, wuv=wuv,
        wout=(wout[:na], wout[na:na + nb], wout[na + nb:]),
        wp=jnp.concatenate([p['c_cmp_pos_k'][l], p['c_cmp_pos_v'][l]], axis=1),
        wproj=_mx(jnp.block([[p['c_cmp_proj_k'][l], zero], [zero, p['c_cmp_proj_v'][l]]])),
        pre13=_mx(p['ffn_pre_w13'][l]), pre2=_mx(p['ffn_pre_w2'][l]),
        post13=_mx(p['ffn_post_w13'][l]), post2=_mx(p['ffn_post_w2'][l]))


def _prompt_mixers(proj, lw, b, t):
    (aq, akv, akvb, aiq, misc, aikb, qcat, bckv, kb, cq, cqr, ccmp, cslc, cslcb, cwin, cwinb) = proj
    o_a = _dsa_prompt(aiq, misc, aq, aikb, akvb, b, t)
    o_b = _mla_prompt(qcat, kb, lw['wuv'], b, t)
    n_chunk = t // CMP_STRIDE
    n_cmp = n_chunk - CMP_LEN // CMP_STRIDE + 1
    psum = _cmp_rows(ccmp, lw['wp'], b, t).reshape(b, n_chunk, 2 * LANES)
    kvc = _cmp_finish(psum, lw['wproj'], n_cmp)
    o_c = _nsa_prompt(cq, cqr, misc, kvc, cslcb, cwinb, b, t)
    nw = min(WINDOW, t)
    rows = (akv.reshape(b, t, -1), misc.reshape(b, t, -1)[..., :IDX_DIM],
            bckv.reshape(b, t, -1)[..., :KV_LORA + D_ROPE], ccmp.reshape(b, t, -1),
            cslc.reshape(b, t, -1), cwin.reshape(b, t, -1)[:, t - nw:])
    return (o_a, o_b, o_c), rows


def _sample_mixers(proj, lw, l, b, dec, pools, state_win, pt, n_pages):
    (aq, akv, akvb, aiq, misc, aikb, qcat, bckv, kb, cq, cqr, ccmp, cslc, cslcb, cwin, cwinb) = proj
    npg = min(PAGES_PER_STEP, n_pages)
    scp, scn = _dsa_scores(aiq, misc, jnp.swapaxes(pools['a_kidx'], 2, 3), l, pt, _new_pages(aikb, b, dec),
                           b, dec, n_pages)
    o_a = _dsa_sample(aq, scp, scn, pools['a_kv'], l, pt, _new_pages(akvb, b, dec), b, dec, n_pages)
    o_b = _mla_sample(qcat, jnp.swapaxes(pools['b_ckv'], 2, 3), l, pt, _new_pages(kb, b, dec), lw['wuv'],
                      b, dec, n_pages)
    total = n_pages * PAGE + dec
    n_cmp = total // CMP_STRIDE - CMP_LEN // CMP_STRIDE + 1
    psum = _cmp_pages(pools['c_cmp'], l, pt, lw['wp'], b, n_pages)
    kvc = _cmp_finish(psum, lw['wproj'], n_cmp)
    ocw, selsteps, wstate = _nsa_select(cq, cqr, kvc, state_win, l, cwin, _new_pages(cwinb, b, dec),
                                        b, dec, n_pages, npg)
    o_c = _nsa_slc(cqr, misc, ocw, selsteps, pools['c_slc'], l, pt, _new_pages(cslcb, b, dec),
                   b, dec, n_pages, npg)
    rows = (akv.reshape(b, dec, -1), misc.reshape(b, dec, -1)[..., :IDX_DIM],
            bckv.reshape(b, dec, -1)[..., :KV_LORA + D_ROPE], ccmp.reshape(b, dec, -1),
            cslc.reshape(b, dec, -1), wstate)
    return (o_a, o_b, o_c), rows


def _trunk(x, tab, qdt, mixers, weights, p):
    b, t, d = x.shape
    x = x.reshape(b * t, d)
    depth = len(weights)
    per_layer = []
    for l, lw in enumerate(weights):
        x = _ffn(x, p['norm_ffn_pre'][l], lw['pre13'], lw['pre2'])
        proj = _inproj(x, p['norm_mix'][l], lw['w_in'], tab, p['b_q_norm'][l], lw['wuq'], lw['wuk'],
                       p['b_kv_norm'][l], qdt)
        o, rows = mixers(proj, lw, l)
        x = _ffn(x, p['norm_ffn_post'][l], lw['post13'], lw['post2'], mix=tuple(zip(o, lw['wout'])),
                 final_g=p['norm_final'] if l == depth - 1 else None)
        per_layer.append(rows)
    stacked = tuple(jnp.stack(r, axis=0) for r in zip(*per_layer))
    return x.reshape(b, t, d), stacked


def kernel(x_prompt, x_sample, cache_a_kv, cache_a_kidx, cache_b_ckv, cache_c_cmp, cache_c_slc,
           state_c_win, page_table, w_in, w_out, b_w_uq, b_w_uk, b_w_uv, b_q_norm, b_kv_norm,
           c_cmp_pos_k, c_cmp_pos_v, c_cmp_proj_k, c_cmp_proj_v,
           ffn_pre_w13, ffn_pre_w2, ffn_post_w13, ffn_post_w2,
           norm_ffn_pre, norm_mix, norm_ffn_post, norm_final):
    p = {'w_in': w_in, 'w_out': w_out, 'b_w_uq': b_w_uq, 'b_w_uk': b_w_uk, 'b_w_uv': b_w_uv,
         'b_q_norm': b_q_norm, 'b_kv_norm': b_kv_norm,
         'c_cmp_pos_k': c_cmp_pos_k, 'c_cmp_pos_v': c_cmp_pos_v,
         'c_cmp_proj_k': c_cmp_proj_k, 'c_cmp_proj_v': c_cmp_proj_v,
         'ffn_pre_w13': ffn_pre_w13, 'ffn_pre_w2': ffn_pre_w2,
         'ffn_post_w13': ffn_post_w13, 'ffn_post_w2': ffn_post_w2,
         'norm_ffn_pre': norm_ffn_pre, 'norm_mix': norm_mix, 'norm_ffn_post': norm_ffn_post,
         'norm_final': norm_final}
    pools = {'a_kv': cache_a_kv, 'a_kidx': cache_a_kidx, 'b_ckv': cache_b_ckv,
             'c_cmp': cache_c_cmp, 'c_slc': cache_c_slc}
    depth = w_in.shape[0]
    weights = [_layer_weights(l, p) for l in range(depth)]
    bp, tp, _ = x_prompt.shape
    bs, dec, _ = x_sample.shape
    n_pages = page_table.shape[1]
    past_len = n_pages * PAGE
    pt = page_table.reshape(-1).astype(I32)

    tab_p = _rope_table(jnp.arange(tp, dtype=I32))
    tab_s = _rope_table(jnp.tile(past_len + jnp.arange(dec, dtype=I32), bs))

    y_p, rows_p = _trunk(x_prompt, tab_p, MXU_DTYPE,
                         lambda proj, lw, l: _prompt_mixers(proj, lw, bp, tp), weights, p)
    y_s, rows_s = _trunk(x_sample, tab_s, F32,
                         lambda proj, lw, l: _sample_mixers(proj, lw, l, bs, dec, pools, state_c_win, pt, n_pages),
                         weights, p)
    out = [y_p, y_s]
    for rp, rs in zip(rows_p, rows_s):
        out += [rp, rs]
    return tuple(out)
```

```python
import functools

import numpy as np
import jax
import jax.numpy as jnp
from jax import lax
from jax.experimental import pallas as pl
from jax.experimental.pallas import tpu as pltpu

HD = 64
A_HEADS = 4
IDX_HEADS = 8
IDX_DIM = 64
DSA_TOPK = 256
B_HEADS = 8
Q_LORA = 256
KV_LORA = 128
D_NOPE = 64
D_ROPE = 32
D_V = 64
C_HEADS = 4
CMP_LEN = 32
CMP_STRIDE = 16
SLC_BLOCK = 64
N_SLC = 16
WINDOW = 512
N_GATES = 3
ROPE_THETA = 10000.0
NORM_EPS = 1e-6
QBLOCK = 128
FORCE_SCORE = 1e9
PAGE = 128

LANES = 128
MXU_DTYPE = jnp.bfloat16
F32 = jnp.float32
I32 = jnp.int32
NEG = -1e30
INT_MIN = -2147483648
VMEM_LIMIT = 56 * 1024 * 1024
PAGES_PER_STEP = 32
KCHUNK = 512

MISC_W = IDX_DIM
MISC_G = IDX_DIM + IDX_HEADS


def _dot(a, b):
    return jnp.dot(a, b, preferred_element_type=F32)


def _dot_nt(a, b):
    return lax.dot_general(a, b, (((1,), (1,)), ((), ())), preferred_element_type=F32)


def _mx(x):
    return x.astype(MXU_DTYPE)


def _rms(x, g):
    r = lax.rsqrt(jnp.mean(x * x, axis=-1, keepdims=True) + NORM_EPS)
    return x * r * g


def _cparams(n_grid):
    return pltpu.CompilerParams(dimension_semantics=("arbitrary",) * n_grid,
                                vmem_limit_bytes=VMEM_LIMIT)


def _lane(shape):
    return lax.broadcasted_iota(I32, shape, len(shape) - 1)


def _row(shape):
    return lax.broadcasted_iota(I32, shape, len(shape) - 2)


def _order_key(score):
    score = jnp.where(score == 0.0, 0.0, score)
    bits = lax.bitcast_convert_type(score, I32)
    return bits ^ ((bits >> 31) & 0x7FFFFFFF)


def _kth_largest(count_ge, k, shape):
    def body(it, t):
        cand = t ^ jnp.left_shift(jnp.int32(1), 31 - it)
        return jnp.where(count_ge(cand) >= k, cand, t)
    return lax.fori_loop(0, 32, body, jnp.full(shape, INT_MIN, I32))


def _tie_cut(count_tie_below, need, nbits, shape):
    def body(it, cut):
        cand = cut | jnp.left_shift(jnp.int32(1), nbits - 1 - it)
        return jnp.where(count_tie_below(cand) <= need, cand, cut)
    return lax.fori_loop(0, nbits, body, jnp.zeros(shape, I32))


def _lane_tree_sum(x):
    while x.shape[-1] > LANES and (x.shape[-1] // LANES) % 2 == 0:
        half = x.shape[-1] // 2
        x = x[:, :half] + x[:, half:]
    return jnp.sum(x, axis=-1, keepdims=True)


def _topk_mask(key, idx, k, nbits, axis=-1):
    rows = (key.shape[0], 1) if axis == -1 else (1, key.shape[1])

    def count(pred):
        return jnp.sum(jnp.where(pred, 1, 0), axis=axis, keepdims=True)

    thr = jnp.maximum(_kth_largest(lambda cand: count(key >= cand), k, rows), INT_MIN + 1)
    above = key > thr
    tied = key == thr
    need = k - count(above)
    cut = _tie_cut(lambda cand: count(tied & (idx < cand)), need, nbits, rows)
    return above | (tied & (idx < cut))


def _softmax_update(s, mask, m_ref, l_ref, acc_ref, v, v_transposed=False):
    m_old = m_ref[...]
    m_new = jnp.maximum(m_old, jnp.max(s, axis=-1, keepdims=True))
    p = jnp.exp(s - m_new)
    if mask is not None:
        p = jnp.where(mask, p, 0.0)
    alpha = jnp.exp(m_old - m_new)
    l_ref[...] = alpha * l_ref[...] + jnp.sum(p, axis=-1, keepdims=True)
    pv = _dot_nt(_mx(p), v) if v_transposed else _dot(_mx(p), v)
    acc_ref[...] = alpha * acc_ref[...] + pv
    m_ref[...] = m_new


def _flash_chunk_t(q, k, vt, mask, m_ref, l_ref, acc_ref):
    s = _dot_nt(k, q)
    if mask is not None:
        s = jnp.where(mask, s, NEG)
    m_old = m_ref[...]
    m_new = jnp.maximum(m_old, jnp.max(s, axis=0, keepdims=True))
    p = jnp.exp(s - m_new)
    if mask is not None:
        p = jnp.where(mask, p, 0.0)
    alpha = jnp.exp(m_old - m_new)
    l_ref[...] = alpha * l_ref[...] + jnp.sum(p, axis=0, keepdims=True)
    acc_ref[...] = alpha * acc_ref[...] + _dot(vt, _mx(p))
    m_ref[...] = m_new


def _chunked_transpose(x, b, t, kc):
    return jnp.swapaxes(x.reshape(b, t // kc, kc, x.shape[-1]), 2, 3)


def _tile_rows(x, n):
    return jnp.concatenate([x] * n, axis=0)


def _ffn_kernel(n_mix, has_final, *refs):
    x_ref = refs[0]
    mix = refs[1:1 + 2 * n_mix]
    g_ref, w1_ref, w3_ref, w2_ref = refs[1 + 2 * n_mix:5 + 2 * n_mix]
    pos = 5 + 2 * n_mix
    gf_ref = refs[pos] if has_final else None
    pos += 1 if has_final else 0
    out_ref, xs, hs, acc = refs[pos:pos + 4]
    f = pl.program_id(1)

    @pl.when(f == 0)
    def _():
        x = x_ref[...]
        for i in range(n_mix):
            x = x + _dot(_mx(mix[2 * i][...]), mix[2 * i + 1][...])
        xs[...] = x
        hs[...] = _mx(_rms(x, g_ref[...]))
        acc[...] = jnp.zeros_like(acc)

    h = hs[...]
    gate = _dot(h, w1_ref[...])
    up = _dot(h, w3_ref[...])
    acc[...] += _dot(_mx(jax.nn.silu(gate) * up), w2_ref[...])

    @pl.when(f == pl.num_programs(1) - 1)
    def _():
        y = xs[...] + 0.5 * acc[...]
        if has_final:
            y = _rms(y, gf_ref[...])
        out_ref[...] = y


def _ffn(x, g, w13, w2, mix=(), final_g=None):
    n, d = x.shape
    d_ff = w2.shape[0]
    tm = min(1024, n)
    tf = 256
    nf = d_ff // tf
    assert n % tm == 0 and d_ff % tf == 0
    in_specs = [pl.BlockSpec((tm, d), lambda i, f: (i, 0))]
    args = [x]
    for o, w in mix:
        in_specs += [pl.BlockSpec((tm, o.shape[1]), lambda i, f: (i, 0)),
                     pl.BlockSpec(w.shape, lambda i, f: (0, 0))]
        args += [o, w]
    in_specs += [pl.BlockSpec((1, d), lambda i, f: (0, 0)),
                 pl.BlockSpec((d, tf), lambda i, f: (0, f)),
                 pl.BlockSpec((d, tf), lambda i, f: (0, f + nf)),
                 pl.BlockSpec((tf, d), lambda i, f: (f, 0))]
    args += [g.reshape(1, d), w13, w13, w2]
    if final_g is not None:
        in_specs.append(pl.BlockSpec((1, d), lambda i, f: (0, 0)))
        args.append(final_g.reshape(1, d))
    return pl.pallas_call(
        functools.partial(_ffn_kernel, len(mix), final_g is not None),
        out_shape=jax.ShapeDtypeStruct((n, d), F32),
        grid=(n // tm, nf),
        in_specs=in_specs,
        out_specs=pl.BlockSpec((tm, d), lambda i, f: (i, 0)),
        scratch_shapes=[pltpu.VMEM((tm, d), F32), pltpu.VMEM((tm, d), MXU_DTYPE),
                        pltpu.VMEM((tm, d), F32)],
        compiler_params=_cparams(2),
        name="ffn",
    )(*args)


C_AQ, C_AKV, C_AIQ, C_MISC, C_BCQ, C_BCKV, C_BKR, C_CQ, C_CCMP, C_CSLC, C_CWIN, C_END = (
    0, 256, 384, 896, 1024, 1280, 1408, 1536, 1792, 1920, 2048, 2176)
T_HH, T_HV, T_R1 = 0, 2, 4


def _rope(x, tab_ref, kind, half):
    cos = tab_ref[:, kind * LANES:(kind + 1) * LANES]
    ssin = tab_ref[:, (kind + 1) * LANES:(kind + 2) * LANES]
    first = (_lane(x.shape) % (2 * half)) < half
    rot = jnp.where(first, pltpu.roll(x, LANES - half, 1), pltpu.roll(x, half, 1))
    return x * cos + rot * ssin


def _split_heads(r):
    low = _lane(r.shape) < HD
    return jnp.where(low, r, 0.0), jnp.where(low, pltpu.roll(r, HD, 1), 0.0)


def _inproj_kernel(x_ref, g_ref, w_ref, tab_ref, qn_ref, wuq_ref, wuk_ref, kvn_ref,
                   aq_ref, akv_ref, akvb_ref, aiq_ref, misc_ref, aikb_ref, qcat_ref,
                   bckv_ref, kb_ref, cq_ref, cqr_ref, ccmp_ref, cslc_ref, cslcb_ref,
                   cwin_ref, cwinb_ref):
    h = _mx(_rms(x_ref[...], g_ref[...]))

    def proj(c0, c1):
        return _dot(h, w_ref[:, c0:c1])

    qscale = HD ** -0.5
    for i in range(A_HEADS // 2):
        r = _rope(proj(C_AQ + LANES * i, C_AQ + LANES * (i + 1)), tab_ref, T_HH, HD // 2) * qscale
        h0, h1 = _split_heads(r)
        aq_ref[2 * i] = h0.astype(aq_ref.dtype)
        aq_ref[2 * i + 1] = h1.astype(aq_ref.dtype)

    akv = _rope(proj(C_AKV, C_AIQ), tab_ref, T_HV, HD // 2)
    akv_ref[...] = akv
    akvb_ref[...] = _mx(akv)

    iscale = IDX_DIM ** -0.5
    for i in range(IDX_HEADS // 2):
        r = _rope(proj(C_AIQ + LANES * i, C_AIQ + LANES * (i + 1)), tab_ref, T_HH, IDX_DIM // 2) * iscale
        h0, h1 = _split_heads(r)
        aiq_ref[2 * i] = h0.astype(aiq_ref.dtype)
        aiq_ref[2 * i + 1] = h1.astype(aiq_ref.dtype)

    misc = _rope(proj(C_MISC, C_BCQ), tab_ref, T_HV, IDX_DIM // 2)
    lane = _lane(misc.shape)
    misc = jnp.where((lane >= MISC_W) & (lane < MISC_G), misc * IDX_HEADS ** -0.5, misc)
    misc = jnp.where((lane >= MISC_G) & (lane < MISC_G + C_HEADS * N_GATES),
                     jax.nn.sigmoid(misc), misc)
    misc_ref[...] = misc
    aikb_ref[...] = _mx(jnp.where(lane < IDX_DIM, misc, 0.0))

    bscale = (D_NOPE + D_ROPE) ** -0.5
    cqn = _mx(_rms(proj(C_BCQ, C_BCKV), qn_ref[...]))
    for p in range(B_HEADS // 2):
        nope = _mx(_dot(cqn, wuq_ref[:, LANES * p:LANES * (p + 1)]))
        lat = _dot(nope, wuk_ref[p]) * bscale
        qcat_ref[2 * p, :, 0:LANES] = lat[:, :LANES].astype(qcat_ref.dtype)
        qcat_ref[2 * p + 1, :, 0:LANES] = lat[:, LANES:].astype(qcat_ref.dtype)
    for hh in range(B_HEADS):
        c0 = B_HEADS * D_NOPE + LANES * hh
        qr = _rope(_dot(cqn, wuq_ref[:, c0:c0 + LANES]), tab_ref, T_R1, D_ROPE // 2) * bscale
        qcat_ref[hh, :, LANES:2 * LANES] = qr.astype(qcat_ref.dtype)

    lat = _rms(proj(C_BCKV, C_BKR), kvn_ref[...])
    kr = _rope(proj(C_BKR, C_CQ), tab_ref, T_R1, D_ROPE // 2)
    bckv_ref[:, 0:LANES] = lat
    bckv_ref[:, LANES:2 * LANES] = kr
    kb_ref[:, 0:LANES] = _mx(lat)
    kb_ref[:, LANES:2 * LANES] = _mx(kr)

    for i in range(C_HEADS // 2):
        z = proj(C_CQ + LANES * i, C_CQ + LANES * (i + 1)) * qscale
        h0, h1 = _split_heads(z)
        cq_ref[2 * i] = h0.astype(cq_ref.dtype)
        cq_ref[2 * i + 1] = h1.astype(cq_ref.dtype)
        h0, h1 = _split_heads(_rope(z, tab_ref, T_HH, HD // 2))
        cqr_ref[2 * i] = h0.astype(cqr_ref.dtype)
        cqr_ref[2 * i + 1] = h1.astype(cqr_ref.dtype)

    ccmp_ref[...] = proj(C_CCMP, C_CSLC)
    cslc = _rope(proj(C_CSLC, C_CWIN), tab_ref, T_HV, HD // 2)
    cslc_ref[...] = cslc
    cslcb_ref[...] = _mx(cslc)
    cwin = _rope(proj(C_CWIN, C_END), tab_ref, T_HV, HD // 2)
    cwin_ref[...] = cwin
    cwinb_ref[...] = _mx(cwin)


def _inproj(x, g, w, tab, qn, wuq, wuk, kvn, qdt):
    n, d = x.shape
    tm = min(1024, n)
    nt = tab.shape[0] // tm
    assert n % tm == 0 and tab.shape[0] % tm == 0
    row = lambda i: (i, 0)
    hrow = lambda i: (0, i, 0)
    const2 = lambda i: (0, 0)
    const3 = lambda i: (0, 0, 0)
    sds = jax.ShapeDtypeStruct
    out_shape = (
        sds((A_HEADS, n, LANES), qdt), sds((n, LANES), F32), sds((n, LANES), MXU_DTYPE),
        sds((IDX_HEADS, n, LANES), qdt), sds((n, LANES), F32), sds((n, LANES), MXU_DTYPE),
        sds((B_HEADS, n, 2 * LANES), qdt), sds((n, 2 * LANES), F32), sds((n, 2 * LANES), MXU_DTYPE),
        sds((C_HEADS, n, LANES), qdt), sds((C_HEADS, n, LANES), qdt),
        sds((n, LANES), F32), sds((n, LANES), F32), sds((n, LANES), MXU_DTYPE),
        sds((n, LANES), F32), sds((n, LANES), MXU_DTYPE))
    out_specs = (
        pl.BlockSpec((A_HEADS, tm, LANES), hrow), pl.BlockSpec((tm, LANES), row), pl.BlockSpec((tm, LANES), row),
        pl.BlockSpec((IDX_HEADS, tm, LANES), hrow), pl.BlockSpec((tm, LANES), row), pl.BlockSpec((tm, LANES), row),
        pl.BlockSpec((B_HEADS, tm, 2 * LANES), hrow), pl.BlockSpec((tm, 2 * LANES), row),
        pl.BlockSpec((tm, 2 * LANES), row),
        pl.BlockSpec((C_HEADS, tm, LANES), hrow), pl.BlockSpec((C_HEADS, tm, LANES), hrow),
        pl.BlockSpec((tm, LANES), row), pl.BlockSpec((tm, LANES), row), pl.BlockSpec((tm, LANES), row),
        pl.BlockSpec((tm, LANES), row), pl.BlockSpec((tm, LANES), row))
    return pl.pallas_call(
        _inproj_kernel,
        out_shape=out_shape,
        grid=(n // tm,),
        in_specs=[pl.BlockSpec((tm, d), row), pl.BlockSpec((1, d), const2),
                  pl.BlockSpec(w.shape, const2),
                  pl.BlockSpec((tm, tab.shape[1]), lambda i: (i % nt, 0)),
                  pl.BlockSpec((1, Q_LORA), const2), pl.BlockSpec(wuq.shape, const2),
                  pl.BlockSpec(wuk.shape, const3), pl.BlockSpec((1, KV_LORA), const2)],
        out_specs=out_specs,
        compiler_params=_cparams(1),
        name="inproj",
    )(x, g.reshape(1, d), w, tab, qn.reshape(1, Q_LORA), wuq, wuk, kvn.reshape(1, KV_LORA))


def _rope_table(pos):
    pos = pos.astype(F32)[:, None]

    def cs(half):
        inv = ROPE_THETA ** (-jnp.arange(half, dtype=F32) / half)
        ang = pos * inv[None, :]
        return jnp.cos(ang), jnp.sin(ang)

    n = pos.shape[0]
    c32, s32 = cs(HD // 2)
    c16, s16 = cs(D_ROPE // 2)
    one = lambda w: jnp.ones((n, w), F32)
    zero = lambda w: jnp.zeros((n, w), F32)
    cols = [jnp.concatenate([c32, c32, c32, c32], 1), jnp.concatenate([-s32, s32, -s32, s32], 1),
            jnp.concatenate([c32, c32, one(HD)], 1), jnp.concatenate([-s32, s32, zero(HD)], 1),
            jnp.concatenate([c16, c16, one(LANES - D_ROPE)], 1),
            jnp.concatenate([-s16, s16, zero(LANES - D_ROPE)], 1)]
    return jnp.concatenate(cols, axis=1)


def _layout_w_in(w):
    sizes = (A_HEADS * HD, HD, HD, IDX_HEADS * IDX_DIM, IDX_HEADS, IDX_DIM, Q_LORA, KV_LORA, D_ROPE,
             C_HEADS * HD, HD, HD, HD, HD, HD, HD, C_HEADS * N_GATES)
    offs = np.concatenate([[0], np.cumsum(sizes)])
    (a_q, a_k, a_v, a_iq, a_iw, a_ik, b_cq, b_ckv, b_kr,
     c_q, c_kc, c_vc, c_ks, c_vs, c_kw, c_vw, c_g) = [w[:, offs[i]:offs[i + 1]] for i in range(len(sizes))]
    z = lambda k: jnp.zeros((w.shape[0], k), w.dtype)
    misc_pad = LANES - IDX_DIM - IDX_HEADS - C_HEADS * N_GATES
    out = jnp.concatenate([a_q, a_k, a_v, a_iq, a_ik, a_iw, c_g, z(misc_pad), b_cq, b_ckv,
                           b_kr, z(LANES - D_ROPE), c_q, c_kc, c_vc, c_ks, c_vs, c_kw, c_vw], axis=1)
    assert out.shape[1] == C_END
    return _mx(out)


def _layout_mla(w_uq, w_uk, w_uv):
    d_qk = D_NOPE + D_ROPE
    uq = w_uq.reshape(Q_LORA, B_HEADS, d_qk)
    nope = uq[:, :, :D_NOPE].reshape(Q_LORA, B_HEADS * D_NOPE)
    rope = jnp.pad(uq[:, :, D_NOPE:], ((0, 0), (0, 0), (0, LANES - D_ROPE))).reshape(Q_LORA, B_HEADS * LANES)
    wuq = _mx(jnp.concatenate([nope, rope], axis=1))
    ukt = jnp.transpose(w_uk, (1, 2, 0))
    uvh = jnp.transpose(w_uv, (1, 0, 2))
    zk = jnp.zeros((D_NOPE, KV_LORA), w_uk.dtype)
    zv = jnp.zeros((KV_LORA, D_V), w_uv.dtype)
    wuk = jnp.stack([jnp.block([[ukt[2 * p], zk], [zk, ukt[2 * p + 1]]]) for p in range(B_HEADS // 2)])
    wuv = jnp.stack([jnp.block([[uvh[2 * p], zv], [zv, uvh[2 * p + 1]]]) for p in range(B_HEADS // 2)])
    return wuq, _mx(wuk), _mx(wuv)


def _dsa_prompt_kernel(n_keep, qi_ref, misc_ref, aq_ref, kidx_ref, akv_ref, akvt_ref, o_ref,
                       keys_ref, m_ref, l_ref, acc_ref):
    i = pl.program_id(1)
    tq = QBLOCK
    kc = keys_ref.shape[1]
    nch = (i * tq) // kc + 1
    qi = qi_ref[...].reshape(IDX_HEADS * tq, LANES)
    misc_t = misc_ref[...].T
    kpos0 = _row((kc, tq))
    qpos = i * tq + _lane((kc, tq))

    def score_chunk(c, carry):
        k = kidx_ref[pl.ds(pl.multiple_of(c * kc, kc), kc), :]
        rel = jnp.maximum(_dot_nt(k, qi), 0.0)
        score = jnp.zeros((kc, tq), F32)
        for hh in range(IDX_HEADS):
            score = score + rel[:, hh * tq:(hh + 1) * tq] * misc_t[MISC_W + hh:MISC_W + hh + 1, :]
        keys_ref[c] = jnp.where((c * kc + kpos0) <= qpos, _order_key(score), INT_MIN)
        return carry

    lax.fori_loop(0, nch, score_chunk, 0)

    def count_where(pred_fn):
        def body(c, cnt):
            hit = jnp.where(pred_fn(c, keys_ref[c]), 1, 0)
            return cnt + jnp.sum(hit.reshape(kc // 8, 8, tq), axis=0)
        cnt = lax.fori_loop(0, nch, body, jnp.zeros((8, tq), I32))
        return jnp.sum(cnt, axis=0, keepdims=True)

    def count_ge(cand):
        return count_where(lambda c, kk: kk >= cand)

    thr = jnp.maximum(_kth_largest(count_ge, n_keep, (1, tq)), INT_MIN + 1)

    @pl.when(jnp.max(count_ge(thr)) > n_keep)
    def _():
        need = n_keep - count_ge(thr + 1)
        cut = _tie_cut(lambda cand: count_where(lambda c, kk: (kk == thr) & ((c * kc + kpos0) < cand)),
                       need, int(kidx_ref.shape[0]).bit_length(), (1, tq))

        def drop(c, carry):
            kk = keys_ref[c]
            keys_ref[c] = jnp.where((kk == thr) & ((c * kc + kpos0) >= cut), INT_MIN, kk)
            return carry

        lax.fori_loop(0, nch, drop, 0)

    m_ref[...] = jnp.full_like(m_ref, NEG)
    l_ref[...] = jnp.zeros_like(l_ref)
    acc_ref[...] = jnp.zeros_like(acc_ref)
    qa = aq_ref[...].reshape(A_HEADS * tq, LANES)

    def attn_chunk(c, carry):
        kv = akv_ref[pl.ds(pl.multiple_of(c * kc, kc), kc), :]
        mask = jnp.concatenate([keys_ref[c] >= thr] * A_HEADS, axis=1)
        _flash_chunk_t(qa, kv, akvt_ref[c], mask, m_ref, l_ref, acc_ref)
        return carry

    lax.fori_loop(0, nch, attn_chunk, 0)
    ot = acc_ref[...] / l_ref[...]
    for hh in range(A_HEADS):
        o_ref[:, hh * HD:(hh + 1) * HD] = ot[:, hh * tq:(hh + 1) * tq].T[:, HD:].astype(o_ref.dtype)


def _dsa_prompt(aiq, misc, aq, aikb, akvb, b, t):
    n = b * t
    nq = t // QBLOCK
    kc = min(KCHUNK, t)
    n_keep = min(DSA_TOPK, t // 4)
    qmap = lambda bb, i: (0, bb * nq + i, 0)
    akvt = _chunked_transpose(akvb, b, t, kc)
    return pl.pallas_call(
        functools.partial(_dsa_prompt_kernel, n_keep),
        out_shape=jax.ShapeDtypeStruct((n, A_HEADS * HD), MXU_DTYPE),
        grid=(b, nq),
        in_specs=[pl.BlockSpec((IDX_HEADS, QBLOCK, LANES), qmap),
                  pl.BlockSpec((QBLOCK, LANES), lambda bb, i: (bb * nq + i, 0)),
                  pl.BlockSpec((A_HEADS, QBLOCK, LANES), qmap),
                  pl.BlockSpec((t, LANES), lambda bb, i: (bb, 0)),
                  pl.BlockSpec((t, LANES), lambda bb, i: (bb, 0)),
                  pl.BlockSpec((None, t // kc, LANES, kc), lambda bb, i: (bb, 0, 0, 0))],
        out_specs=pl.BlockSpec((QBLOCK, A_HEADS * HD), lambda bb, i: (bb * nq + i, 0)),
        scratch_shapes=[pltpu.VMEM((t // kc, kc, QBLOCK), I32),
                        pltpu.VMEM((1, A_HEADS * QBLOCK), F32), pltpu.VMEM((1, A_HEADS * QBLOCK), F32),
                        pltpu.VMEM((LANES, A_HEADS * QBLOCK), F32)],
        compiler_params=_cparams(2),
        name="dsa_prompt",
    )(aiq, misc, aq, aikb, akvb, akvt)


def _mla_prompt_kernel(q_ref, kb_ref, latt_ref, wuv_ref, o_ref, m_ref, l_ref, acc_ref):
    i = pl.program_id(1)
    tq = QBLOCK
    kc = latt_ref.shape[2]
    rows = B_HEADS * tq
    q = q_ref[...].reshape(rows, 2 * LANES)
    m_ref[...] = jnp.full_like(m_ref, NEG)
    l_ref[...] = jnp.zeros_like(l_ref)
    acc_ref[...] = jnp.zeros_like(acc_ref)

    def chunk(c, masked):
        k = kb_ref[pl.ds(pl.multiple_of(c * kc, kc), kc), :]
        mask = None
        if masked:
            mask = (c * kc + _row((kc, rows))) <= (i * tq + _lane((kc, rows)) % tq)
        _flash_chunk_t(q, k, latt_ref[c], mask, m_ref, l_ref, acc_ref)

    nfull = (i * tq) // kc

    def body(c, carry):
        chunk(c, False)
        return carry

    lax.fori_loop(0, nfull, body, 0)
    chunk(nfull, True)
    ot = acc_ref[...] / l_ref[...]
    for p in range(B_HEADS // 2):
        pair = jnp.concatenate([ot[:, 2 * p * tq:(2 * p + 1) * tq].T, ot[:, (2 * p + 1) * tq:(2 * p + 2) * tq].T],
                               axis=1)
        o_ref[:, p * LANES:(p + 1) * LANES] = _dot(_mx(pair), wuv_ref[p]).astype(o_ref.dtype)


def _mla_prompt(qcat, kb, wuv, b, t):
    n = b * t
    nq = t // QBLOCK
    kc = min(KCHUNK, t)
    latt = _chunked_transpose(kb[:, :KV_LORA], b, t, kc)
    return pl.pallas_call(
        _mla_prompt_kernel,
        out_shape=jax.ShapeDtypeStruct((n, B_HEADS * D_V), MXU_DTYPE),
        grid=(b, nq),
        in_specs=[pl.BlockSpec((B_HEADS, QBLOCK, 2 * LANES), lambda bb, i: (0, bb * nq + i, 0)),
                  pl.BlockSpec((t, 2 * LANES), lambda bb, i: (bb, 0)),
                  pl.BlockSpec((None, t // kc, KV_LORA, kc), lambda bb, i: (bb, 0, 0, 0)),
                  pl.BlockSpec(wuv.shape, lambda bb, i: (0, 0, 0))],
        out_specs=pl.BlockSpec((QBLOCK, B_HEADS * D_V), lambda bb, i: (bb * nq + i, 0)),
        scratch_shapes=[pltpu.VMEM((1, B_HEADS * QBLOCK), F32), pltpu.VMEM((1, B_HEADS * QBLOCK), F32),
                        pltpu.VMEM((KV_LORA, B_HEADS * QBLOCK), F32)],
        compiler_params=_cparams(2),
        name="mla_prompt",
    )(qcat, kb, latt, wuv)


def _chunk_sums(x, wp):
    x3 = x.reshape(x.shape[0] // CMP_STRIDE, CMP_STRIDE, LANES)
    p0 = jnp.sum(x3 * wp[None, :CMP_STRIDE], axis=1)
    p1 = jnp.sum(x3 * wp[None, CMP_STRIDE:], axis=1)
    return jnp.concatenate([p0, p1], axis=1)


def _cmp_rows_kernel(x_ref, wp_ref, p_ref):
    p_ref[...] = _chunk_sums(x_ref[...], wp_ref[...])


def _cmp_rows(rows, wp, b, t):
    tr = min(1024, t)
    return pl.pallas_call(
        _cmp_rows_kernel,
        out_shape=jax.ShapeDtypeStruct((b * t // CMP_STRIDE, 2 * LANES), F32),
        grid=(b * t // tr,),
        in_specs=[pl.BlockSpec((tr, LANES), lambda i: (i, 0)), pl.BlockSpec(wp.shape, lambda i: (0, 0))],
        out_specs=pl.BlockSpec((tr // CMP_STRIDE, 2 * LANES), lambda i: (i, 0)),
        compiler_params=_cparams(1),
        name="cmp_rows",
    )(rows, wp)


def _cmp_pages_kernel(npg, pt_ref, *refs):
    pages, wp_ref, p_ref = refs[:npg], refs[npg], refs[npg + 1]
    cpp = PAGE // CMP_STRIDE
    for g in range(npg):
        p_ref[g * cpp:(g + 1) * cpp, :] = _chunk_sums(pages[g][...], wp_ref[...])


def _page_specs(npg, width, layer, n_pages, transposed=False):
    block = (None, None, width, PAGE) if transposed else (None, None, PAGE, width)

    def spec(g):
        return pl.BlockSpec(block, lambda bb, s, pt: (layer, pt[bb * n_pages + s * npg + g], 0, 0))
    return [spec(g) for g in range(npg)]


def _cmp_pages(pool, layer, pt, wp, b, n_pages):
    npg = min(PAGES_PER_STEP, n_pages)
    cpp = PAGE // CMP_STRIDE
    grid_spec = pltpu.PrefetchScalarGridSpec(
        num_scalar_prefetch=1, grid=(b, n_pages // npg),
        in_specs=_page_specs(npg, LANES, layer, n_pages) + [pl.BlockSpec(wp.shape, lambda bb, s, pt: (0, 0))],
        out_specs=pl.BlockSpec((None, npg * cpp, 2 * LANES), lambda bb, s, pt: (bb, s, 0)))
    return pl.pallas_call(
        functools.partial(_cmp_pages_kernel, npg),
        out_shape=jax.ShapeDtypeStruct((b, n_pages * cpp, 2 * LANES), F32),
        grid_spec=grid_spec,
        compiler_params=_cparams(2),
        name="cmp_pages",
    )(pt, *([pool] * npg), wp)


def _cmp_finish_kernel(n_cmp, p0_ref, p1_ref, w_ref, o_ref):
    blk = p0_ref[:, :LANES] + p1_ref[...]
    out = _dot(_mx(blk), w_ref[...])
    o_ref[...] = jnp.where(_row(out.shape) < n_cmp, out, 0.0).astype(o_ref.dtype)


def _cmp_finish(p, wproj, n_cmp):
    b, n_chunk, _ = p.shape
    p1s = jnp.pad(p[:, 1:, LANES:], ((0, 0), (0, 1), (0, 0)))
    return pl.pallas_call(
        functools.partial(_cmp_finish_kernel, n_cmp),
        out_shape=jax.ShapeDtypeStruct((b, n_chunk, LANES), MXU_DTYPE),
        grid=(b,),
        in_specs=[pl.BlockSpec((None, n_chunk, 2 * LANES), lambda bb: (bb, 0, 0)),
                  pl.BlockSpec((None, n_chunk, LANES), lambda bb: (bb, 0, 0)),
                  pl.BlockSpec(wproj.shape, lambda bb: (0, 0))],
        out_specs=pl.BlockSpec((None, n_chunk, LANES), lambda bb: (bb, 0, 0)),
        compiler_params=_cparams(1),
        name="cmp_finish",
    )(p, p1s, wproj)


def _band_matrix(n_rows, n_cols):
    r = SLC_BLOCK // CMP_STRIDE
    c = CMP_LEN // CMP_STRIDE
    i = np.arange(n_rows)[:, None]
    j = np.arange(n_cols)[None, :]
    return jnp.asarray((i >= r * j - (c - 1)) & (i <= r * j + r - 1), MXU_DTYPE)


def _importance(pc, band):
    hi = pc.astype(jnp.bfloat16)
    r1 = pc - hi.astype(F32)
    mid = r1.astype(jnp.bfloat16)
    lo = (r1 - mid.astype(F32)).astype(jnp.bfloat16)
    band = band.astype(jnp.bfloat16)
    return _dot(hi, band) + _dot(mid, band) + _dot(lo, band)


def _select_blocks(imp, j, qpos, n_slc, n_top, axis=-1):
    cur = qpos // SLC_BLOCK
    forced = (j == 0) | (j == cur) | (j == cur - 1)
    adm = (j * SLC_BLOCK <= qpos) & (j < n_slc)
    score = jnp.where(forced, FORCE_SCORE, jnp.where(adm, imp, -jnp.inf))
    key = jnp.where(adm, _order_key(score), INT_MIN)
    return jnp.where(_topk_mask(key, j, n_top, int(imp.shape[axis]).bit_length(), axis), 1.0, 0.0)


def _masked_softmax(s, mask):
    s = jnp.where(mask, s, NEG)
    m = jnp.max(s, axis=-1, keepdims=True)
    e = jnp.where(mask, jnp.exp(s - m), 0.0)
    return e / jnp.maximum(jnp.sum(e, axis=-1, keepdims=True), jnp.finfo(F32).tiny)


def _nsa_prompt_kernel(n_cmp, n_slc, n_top, cq_ref, cqr_ref, misc_ref, kvc_ref, band_ref,
                       slc_ref, slct_ref, win_ref, o_ref, m_ref, l_ref, acc_ref):
    i = pl.program_id(1)
    tq = QBLOCK
    t = slc_ref.shape[0]
    kc = min(KCHUNK, t)
    q = cq_ref[...].reshape(C_HEADS * tq, LANES)
    qr = cqr_ref[...].reshape(C_HEADS * tq, LANES)

    kvc = kvc_ref[...]
    ncp = kvc.shape[0]
    qpos_c = i * tq + _row((tq, ncp))
    n_id = _lane((tq, ncp))
    cmask = (n_id * CMP_STRIDE + CMP_LEN - 1 <= qpos_c) & (n_id < n_cmp)
    p_cmp = _masked_softmax(_dot_nt(q, kvc), _tile_rows(cmask, C_HEADS))
    o_cmp = _dot(_mx(p_cmp), kvc)
    pc = p_cmp[0:tq]
    for hh in range(1, C_HEADS):
        pc = pc + p_cmp[hh * tq:(hh + 1) * tq]

    imp = _importance(pc, band_ref[...])
    imp_t = imp.T
    sel_t = _select_blocks(imp_t, _row(imp_t.shape), i * tq + _lane(imp_t.shape), n_slc, n_top, axis=0)
    sel_t = sel_t.astype(jnp.bfloat16)

    m_ref[...] = jnp.full_like(m_ref, NEG)
    l_ref[...] = jnp.zeros_like(l_ref)
    acc_ref[...] = jnp.zeros_like(acc_ref)
    nsl = sel_t.shape[0]
    qpos = i * tq + _lane((kc, tq))

    def slc_chunk(c, carry):
        kv = slc_ref[pl.ds(pl.multiple_of(c * kc, kc), kc), :]
        kblock = (c * kc + _row((kc, nsl))) // SLC_BLOCK
        expand = jnp.where(_lane((kc, nsl)) == kblock, 1.0, 0.0).astype(jnp.bfloat16)
        visible = (c * kc + _row((kc, tq))) <= qpos
        picked = jnp.where(visible, _dot(expand, sel_t), 0.0) > 0.5
        _flash_chunk_t(qr, kv, slct_ref[c], jnp.concatenate([picked] * C_HEADS, axis=1), m_ref, l_ref, acc_ref)
        return carry

    lax.fori_loop(0, (i * tq) // kc + 1, slc_chunk, 0)
    ot = acc_ref[...] / jnp.maximum(l_ref[...], jnp.finfo(F32).tiny)
    o_slc = jnp.concatenate([ot[:, hh * tq:(hh + 1) * tq].T for hh in range(C_HEADS)], axis=0)

    nw = min(WINDOW + tq, t)
    start = jnp.maximum(i * tq + tq - nw, 0)
    rows = win_ref[pl.ds(pl.multiple_of(start, tq), nw), :]
    dist = (i * tq + _row((tq, nw))) - (start + _lane((tq, nw)))
    wmask = _tile_rows((dist >= 0) & (dist < WINDOW), C_HEADS)
    o_win = _dot(_mx(_masked_softmax(_dot_nt(qr, rows), wmask)), rows)

    misc = misc_ref[...]
    for hh in range(C_HEADS):
        g = [misc[:, MISC_G + N_GATES * hh + k:MISC_G + N_GATES * hh + k + 1] for k in range(N_GATES)]
        sl = slice(hh * tq, (hh + 1) * tq)
        o = g[0] * o_cmp[sl] + g[1] * o_slc[sl] + g[2] * o_win[sl]
        o_ref[:, hh * HD:(hh + 1) * HD] = o[:, HD:].astype(o_ref.dtype)


def _nsa_prompt(cq, cqr, misc, kvc, slcb, winb, b, t):
    n = b * t
    nq = t // QBLOCK
    n_chunk = t // CMP_STRIDE
    n_cmp = n_chunk - CMP_LEN // CMP_STRIDE + 1
    n_slc = -(-t // SLC_BLOCK)
    n_top = min(N_SLC, n_slc)
    nsl = -(-n_slc // LANES) * LANES
    band = _band_matrix(n_chunk, nsl)
    kc = min(KCHUNK, t)
    qmap = lambda bb, i: (0, bb * nq + i, 0)
    return pl.pallas_call(
        functools.partial(_nsa_prompt_kernel, n_cmp, n_slc, n_top),
        out_shape=jax.ShapeDtypeStruct((n, C_HEADS * HD), MXU_DTYPE),
        grid=(b, nq),
        in_specs=[pl.BlockSpec((C_HEADS, QBLOCK, LANES), qmap),
                  pl.BlockSpec((C_HEADS, QBLOCK, LANES), qmap),
                  pl.BlockSpec((QBLOCK, LANES), lambda bb, i: (bb * nq + i, 0)),
                  pl.BlockSpec((None, n_chunk, LANES), lambda bb, i: (bb, 0, 0)),
                  pl.BlockSpec(band.shape, lambda bb, i: (0, 0)),
                  pl.BlockSpec((t, LANES), lambda bb, i: (bb, 0)),
                  pl.BlockSpec((None, t // kc, LANES, kc), lambda bb, i: (bb, 0, 0, 0)),
                  pl.BlockSpec((t, LANES), lambda bb, i: (bb, 0))],
        out_specs=pl.BlockSpec((QBLOCK, C_HEADS * HD), lambda bb, i: (bb * nq + i, 0)),
        scratch_shapes=[pltpu.VMEM((1, C_HEADS * QBLOCK), F32), pltpu.VMEM((1, C_HEADS * QBLOCK), F32),
                        pltpu.VMEM((LANES, C_HEADS * QBLOCK), F32)],
        compiler_params=_cparams(2),
        name="nsa_prompt",
    )(cq, cqr, misc, kvc, band, slcb, _chunked_transpose(slcb, b, t, kc), winb)


def _stack_heads(ref):
    x = ref[...]
    return _mx(x.reshape(x.shape[0] * x.shape[1], x.shape[2]))


def _new_mask(shape, dec):
    return (_lane(shape) <= _row(shape) % dec) & (_lane(shape) < dec)


def _dsa_scores_kernel(npg, pt_ref, *refs):
    qi_ref, misc_ref = refs[0], refs[1]
    pages = refs[2:2 + npg]
    newk_ref, scp_ref, scn_ref, kbuf = refs[2 + npg:6 + npg]
    dec = misc_ref.shape[0]
    q = _stack_heads(qi_ref)[:, :IDX_DIM]
    misc = misc_ref[...]

    def weighted(rel):
        score = jnp.zeros((dec, rel.shape[1]), F32)
        for hh in range(IDX_HEADS):
            score = score + rel[hh * dec:(hh + 1) * dec] * misc[:, MISC_W + hh:MISC_W + hh + 1]
        return score

    for g in range(npg):
        kbuf[:, g * PAGE:(g + 1) * PAGE] = _mx(pages[g][...])
    scp_ref[...] = weighted(jnp.maximum(_dot(q, kbuf[...]), 0.0))

    @pl.when(pl.program_id(1) == 0)
    def _():
        sc = weighted(jnp.maximum(_dot_nt(q, newk_ref[:, :IDX_DIM]), 0.0))
        scn_ref[...] = jnp.where(_new_mask(sc.shape, dec), sc, -jnp.inf)


def _dsa_scores(aiq, misc, pool, layer, pt, newk, b, dec, n_pages):
    npg = min(PAGES_PER_STEP, n_pages)
    grid_spec = pltpu.PrefetchScalarGridSpec(
        num_scalar_prefetch=1, grid=(b, n_pages // npg),
        in_specs=[pl.BlockSpec((IDX_HEADS, dec, LANES), lambda bb, s, pt: (0, bb, 0)),
                  pl.BlockSpec((dec, LANES), lambda bb, s, pt: (bb, 0))]
        + _page_specs(npg, IDX_DIM, layer, n_pages, transposed=True)
        + [pl.BlockSpec((None, PAGE, LANES), lambda bb, s, pt: (bb, 0, 0))],
        out_specs=(pl.BlockSpec((None, dec, npg * PAGE), lambda bb, s, pt: (bb, 0, s)),
                   pl.BlockSpec((None, dec, LANES), lambda bb, s, pt: (bb, 0, 0))),
        scratch_shapes=[pltpu.VMEM((IDX_DIM, npg * PAGE), MXU_DTYPE)])
    return pl.pallas_call(
        functools.partial(_dsa_scores_kernel, npg),
        out_shape=(jax.ShapeDtypeStruct((b, dec, n_pages * PAGE), F32),
                   jax.ShapeDtypeStruct((b, dec, LANES), F32)),
        grid_spec=grid_spec,
        compiler_params=_cparams(2),
        name="dsa_scores",
    )(pt, aiq, misc, *([pool] * npg), newk)


def _dsa_sample_kernel(npg, n_keep, pt_ref, *refs):
    aq_ref, scfull_ref, scn_ref, scstep_ref = refs[:4]
    pages = refs[4:4 + npg]
    newkv_ref, o_ref, thr_ref, cut_ref, m_ref, l_ref, acc_ref, kvbuf = refs[4 + npg:12 + npg]
    dec = scn_ref.shape[0]
    past_len = scfull_ref.shape[1]
    s_id = pl.program_id(1)
    q = _stack_heads(aq_ref)

    def picked(key, pos):
        thr = thr_ref[:, 0:1]
        return (key > thr) | ((key == thr) & (pos < cut_ref[:, 0:1]))

    @pl.when(s_id == 0)
    def _():
        new_ok = _new_mask((dec, LANES), dec)
        key_new = jnp.where(new_ok, _order_key(scn_ref[...]), INT_MIN)
        key_all = _order_key(scfull_ref[...])
        pos_all = _lane(key_all.shape)
        pos_new = past_len + _lane(key_new.shape)

        def count(pred_all, pred_new):
            return (_lane_tree_sum(jnp.where(pred_all, 1, 0))
                    + jnp.sum(jnp.where(pred_new, 1, 0), axis=-1, keepdims=True))

        thr = jnp.maximum(_kth_largest(lambda c: count(key_all >= c, key_new >= c), n_keep, (dec, 1)),
                          INT_MIN + 1)
        thr_ref[...] = jnp.broadcast_to(thr, thr_ref.shape)
        nbits = int(past_len + LANES).bit_length()
        cut_ref[...] = jnp.full(cut_ref.shape, (1 << nbits) - 1, I32)

        @pl.when(jnp.max(count(key_all >= thr, key_new >= thr)) > n_keep)
        def _():
            need = n_keep - count(key_all > thr, key_new > thr)
            cut = _tie_cut(lambda c: count((key_all == thr) & (pos_all < c), (key_new == thr) & (pos_new < c)),
                           need, nbits, (dec, 1))
            cut_ref[...] = jnp.broadcast_to(cut, cut_ref.shape)

        m_ref[...] = jnp.full_like(m_ref, NEG)
        l_ref[...] = jnp.zeros_like(l_ref)
        acc_ref[...] = jnp.zeros_like(acc_ref)
        kv = newkv_ref[...]
        mask = _tile_rows(picked(key_new, pos_new), A_HEADS)
        _softmax_update(jnp.where(mask, _dot_nt(q, kv), NEG), mask, m_ref, l_ref, acc_ref, kv)

    for g in range(npg):
        kvbuf[g * PAGE:(g + 1) * PAGE, :] = _mx(pages[g][...])
    kv = kvbuf[...]
    key_step = _order_key(scstep_ref[...])
    mask = _tile_rows(picked(key_step, s_id * (npg * PAGE) + _lane(key_step.shape)), A_HEADS)
    _softmax_update(jnp.where(mask, _dot_nt(q, kv), NEG), mask, m_ref, l_ref, acc_ref, kv)

    @pl.when(s_id == pl.num_programs(1) - 1)
    def _():
        o = acc_ref[...] / l_ref[...]
        for hh in range(A_HEADS):
            o_ref[:, hh * HD:(hh + 1) * HD] = o[hh * dec:(hh + 1) * dec, HD:]


def _dsa_sample(aq, scp, scn, pool, layer, pt, newkv, b, dec, n_pages):
    npg = min(PAGES_PER_STEP, n_pages)
    n_keep = min(DSA_TOPK, (n_pages * PAGE + dec) // 4)
    grid_spec = pltpu.PrefetchScalarGridSpec(
        num_scalar_prefetch=1, grid=(b, n_pages // npg),
        in_specs=[pl.BlockSpec((A_HEADS, dec, LANES), lambda bb, s, pt: (0, bb, 0)),
                  pl.BlockSpec((None, dec, n_pages * PAGE), lambda bb, s, pt: (bb, 0, 0)),
                  pl.BlockSpec((None, dec, LANES), lambda bb, s, pt: (bb, 0, 0)),
                  pl.BlockSpec((None, dec, npg * PAGE), lambda bb, s, pt: (bb, 0, s))]
        + _page_specs(npg, 2 * HD, layer, n_pages)
        + [pl.BlockSpec((None, PAGE, LANES), lambda bb, s, pt: (bb, 0, 0))],
        out_specs=pl.BlockSpec((dec, A_HEADS * HD), lambda bb, s, pt: (bb, 0)),
        scratch_shapes=[pltpu.VMEM((dec, LANES), I32), pltpu.VMEM((dec, LANES), I32),
                        pltpu.VMEM((A_HEADS * dec, 1), F32), pltpu.VMEM((A_HEADS * dec, 1), F32),
                        pltpu.VMEM((A_HEADS * dec, LANES), F32),
                        pltpu.VMEM((npg * PAGE, LANES), MXU_DTYPE)])
    return pl.pallas_call(
        functools.partial(_dsa_sample_kernel, npg, n_keep),
        out_shape=jax.ShapeDtypeStruct((b * dec, A_HEADS * HD), F32),
        grid_spec=grid_spec,
        compiler_params=_cparams(2),
        name="dsa_sample",
    )(pt, aq, scp, scn, scp, *([pool] * npg), newkv)


def _mla_sample_kernel(npg, pt_ref, *refs):
    q_ref = refs[0]
    pages = refs[1:1 + npg]
    newkb_ref, wuv_ref, o_ref, m_ref, l_ref, acc_ref, kbuf = refs[1 + npg:8 + npg]
    dec = o_ref.shape[0]
    s_id = pl.program_id(1)
    q = _stack_heads(q_ref)
    wk = KV_LORA + D_ROPE

    @pl.when(s_id == 0)
    def _():
        kbuf[...] = jnp.zeros_like(kbuf)
        m_ref[...] = jnp.full_like(m_ref, NEG)
        l_ref[...] = jnp.zeros_like(l_ref)
        acc_ref[...] = jnp.zeros_like(acc_ref)
        k = newkb_ref[...]
        mask = _new_mask((B_HEADS * dec, PAGE), dec)
        _softmax_update(jnp.where(mask, _dot_nt(q, k), NEG), mask, m_ref, l_ref, acc_ref, k[:, :LANES])

    for g in range(npg):
        kbuf[0:wk, g * PAGE:(g + 1) * PAGE] = _mx(pages[g][...])
    _softmax_update(_dot(q, kbuf[...]), None, m_ref, l_ref, acc_ref, kbuf[0:KV_LORA, :], v_transposed=True)

    @pl.when(s_id == pl.num_programs(1) - 1)
    def _():
        o = _mx(acc_ref[...] / l_ref[...])
        for p in range(B_HEADS // 2):
            pair = jnp.concatenate([o[2 * p * dec:(2 * p + 1) * dec], o[(2 * p + 1) * dec:(2 * p + 2) * dec]], axis=1)
            o_ref[:, p * LANES:(p + 1) * LANES] = _dot(pair, wuv_ref[p])


def _mla_sample(qcat, pool, layer, pt, newkb, wuv, b, dec, n_pages):
    npg = min(PAGES_PER_STEP, n_pages)
    grid_spec = pltpu.PrefetchScalarGridSpec(
        num_scalar_prefetch=1, grid=(b, n_pages // npg),
        in_specs=[pl.BlockSpec((B_HEADS, dec, 2 * LANES), lambda bb, s, pt: (0, bb, 0))]
        + _page_specs(npg, KV_LORA + D_ROPE, layer, n_pages, transposed=True)
        + [pl.BlockSpec((None, PAGE, 2 * LANES), lambda bb, s, pt: (bb, 0, 0)),
           pl.BlockSpec(wuv.shape, lambda bb, s, pt: (0, 0, 0))],
        out_specs=pl.BlockSpec((dec, B_HEADS * D_V), lambda bb, s, pt: (bb, 0)),
        scratch_shapes=[pltpu.VMEM((B_HEADS * dec, 1), F32), pltpu.VMEM((B_HEADS * dec, 1), F32),
                        pltpu.VMEM((B_HEADS * dec, LANES), F32),
                        pltpu.VMEM((2 * LANES, npg * PAGE), MXU_DTYPE)])
    return pl.pallas_call(
        functools.partial(_mla_sample_kernel, npg),
        out_shape=jax.ShapeDtypeStruct((b * dec, B_HEADS * D_V), F32),
        grid_spec=grid_spec,
        compiler_params=_cparams(2),
        name="mla_sample",
    )(pt, qcat, *([pool] * npg), newkb, wuv)


def _nsa_select_kernel(n_cmp, n_slc, n_top, past_len, bps, cq_ref, cqr_ref, kvc_ref, band_ref,
                       state_ref, neww_ref, newwb_ref, ocw_ref, sel_ref, wstate_ref):
    dec = neww_ref.shape[0]
    q = _stack_heads(cq_ref)
    qr = _stack_heads(cqr_ref)
    rows_q = C_HEADS * dec

    kvc = kvc_ref[...]
    ncp = kvc.shape[0]
    qpos_c = past_len + _row((dec, ncp)) % dec
    n_id = _lane((dec, ncp))
    cmask = (n_id * CMP_STRIDE + CMP_LEN - 1 <= qpos_c) & (n_id < n_cmp)
    p_cmp = _masked_softmax(_dot_nt(q, kvc), _tile_rows(cmask, C_HEADS))
    ocw_ref[0] = _dot(_mx(p_cmp), kvc)
    pc = p_cmp[0:dec]
    for hh in range(1, C_HEADS):
        pc = pc + p_cmp[hh * dec:(hh + 1) * dec]

    imp = _importance(pc, band_ref[...])
    sel = _select_blocks(imp, _lane(imp.shape), past_len + _row(imp.shape), n_slc, n_top)
    for s in range(sel_ref.shape[0]):
        tile = sel[:, (s * bps // LANES) * LANES:(s * bps // LANES + 1) * LANES]
        shift = (s * bps) % LANES
        sel_ref[s] = pltpu.roll(tile, LANES - shift, 1) if shift else tile

    state = state_ref[...]
    nst = state.shape[0]
    sb = _mx(state)
    nb = newwb_ref[...]
    trow = _row((rows_q, nst)) % dec
    dist = nst + trow - _lane((rows_q, nst))
    m1 = (dist >= 0) & (dist < WINDOW)
    m2 = _new_mask((rows_q, PAGE), dec)
    s1 = jnp.where(m1, _dot_nt(qr, sb), NEG)
    s2 = jnp.where(m2, _dot_nt(qr, nb), NEG)
    m = jnp.maximum(jnp.max(s1, axis=-1, keepdims=True), jnp.max(s2, axis=-1, keepdims=True))
    e1 = jnp.where(m1, jnp.exp(s1 - m), 0.0)
    e2 = jnp.where(m2, jnp.exp(s2 - m), 0.0)
    den = jnp.maximum(jnp.sum(e1, axis=-1, keepdims=True) + jnp.sum(e2, axis=-1, keepdims=True),
                      jnp.finfo(F32).tiny)
    ocw_ref[1] = _dot(_mx(e1 / den), sb) + _dot(_mx(e2 / den), nb)

    wstate_ref[0:nst - dec, :] = state[dec:, :]
    wstate_ref[nst - dec:nst, :] = neww_ref[...]


def _nsa_select(cq, cqr, kvc, state, layer, neww, newwb, b, dec, n_pages, npg):
    past_len = n_pages * PAGE
    total = past_len + dec
    n_chunk = kvc.shape[1]
    n_cmp = total // CMP_STRIDE - CMP_LEN // CMP_STRIDE + 1
    n_slc = -(-total // SLC_BLOCK)
    n_top = min(N_SLC, n_slc)
    bps = npg * PAGE // SLC_BLOCK
    nsteps = n_pages // npg
    nsl = -(-max(n_slc, nsteps * bps) // LANES) * LANES
    band = _band_matrix(n_chunk, nsl)
    nst = state.shape[2]
    return pl.pallas_call(
        functools.partial(_nsa_select_kernel, n_cmp, n_slc, n_top, past_len, bps),
        out_shape=(jax.ShapeDtypeStruct((b, 2, C_HEADS * dec, LANES), F32),
                   jax.ShapeDtypeStruct((b, nsteps, dec, LANES), F32),
                   jax.ShapeDtypeStruct((b, nst, LANES), F32)),
        grid=(b,),
        in_specs=[pl.BlockSpec((C_HEADS, dec, LANES), lambda bb: (0, bb, 0)),
                  pl.BlockSpec((C_HEADS, dec, LANES), lambda bb: (0, bb, 0)),
                  pl.BlockSpec((None, n_chunk, LANES), lambda bb: (bb, 0, 0)),
                  pl.BlockSpec(band.shape, lambda bb: (0, 0)),
                  pl.BlockSpec((None, None, nst, LANES), lambda bb: (layer, bb, 0, 0)),
                  pl.BlockSpec((dec, LANES), lambda bb: (bb, 0)),
                  pl.BlockSpec((None, PAGE, LANES), lambda bb: (bb, 0, 0))],
        out_specs=(pl.BlockSpec((None, 2, C_HEADS * dec, LANES), lambda bb: (bb, 0, 0, 0)),
                   pl.BlockSpec((None, nsteps, dec, LANES), lambda bb: (bb, 0, 0, 0)),
                   pl.BlockSpec((None, nst, LANES), lambda bb: (bb, 0, 0))),
        compiler_params=_cparams(1),
        name="nsa_select",
    )(cq, cqr, kvc, band, state, neww, newwb)


def _nsa_slc_kernel(npg, pt_ref, *refs):
    cqr_ref, misc_ref, ocw_ref, sel_ref = refs[:4]
    pages = refs[4:4 + npg]
    newb_ref, o_ref, m_ref, l_ref, acc_ref, kvbuf = refs[4 + npg:10 + npg]
    dec = misc_ref.shape[0]
    s_id = pl.program_id(1)
    qr = _stack_heads(cqr_ref)

    @pl.when(s_id == 0)
    def _():
        m_ref[...] = jnp.full_like(m_ref, NEG)
        l_ref[...] = jnp.zeros_like(l_ref)
        acc_ref[...] = jnp.zeros_like(acc_ref)
        kv = newb_ref[...]
        mask = _new_mask((C_HEADS * dec, PAGE), dec)
        _softmax_update(jnp.where(mask, _dot_nt(qr, kv), NEG), mask, m_ref, l_ref, acc_ref, kv)

    sel = sel_ref[...]
    low = _lane((dec, PAGE)) < SLC_BLOCK
    masks = []
    for g in range(npg):
        kvbuf[g * PAGE:(g + 1) * PAGE, :] = _mx(pages[g][...])
        masks.append(jnp.where(low, sel[:, 2 * g:2 * g + 1], sel[:, 2 * g + 1:2 * g + 2]) > 0.5)
    mask = _tile_rows(jnp.concatenate(masks, axis=1), C_HEADS)
    kv = kvbuf[...]
    _softmax_update(jnp.where(mask, _dot_nt(qr, kv), NEG), mask, m_ref, l_ref, acc_ref, kv)

    @pl.when(s_id == pl.num_programs(1) - 1)
    def _():
        o_slc = acc_ref[...] / jnp.maximum(l_ref[...], jnp.finfo(F32).tiny)
        misc = misc_ref[...]
        o_cmp = ocw_ref[0]
        o_win = ocw_ref[1]
        for hh in range(C_HEADS):
            g = [misc[:, MISC_G + N_GATES * hh + k:MISC_G + N_GATES * hh + k + 1] for k in range(N_GATES)]
            sl = slice(hh * dec, (hh + 1) * dec)
            o = g[0] * o_cmp[sl] + g[1] * o_slc[sl] + g[2] * o_win[sl]
            o_ref[:, hh * HD:(hh + 1) * HD] = o[:, HD:]


def _nsa_slc(cqr, misc, ocw, selsteps, pool, layer, pt, newb, b, dec, n_pages, npg):
    grid_spec = pltpu.PrefetchScalarGridSpec(
        num_scalar_prefetch=1, grid=(b, n_pages // npg),
        in_specs=[pl.BlockSpec((C_HEADS, dec, LANES), lambda bb, s, pt: (0, bb, 0)),
                  pl.BlockSpec((dec, LANES), lambda bb, s, pt: (bb, 0)),
                  pl.BlockSpec((None, 2, C_HEADS * dec, LANES), lambda bb, s, pt: (bb, 0, 0, 0)),
                  pl.BlockSpec((None, None, dec, LANES), lambda bb, s, pt: (bb, s, 0, 0))]
        + _page_specs(npg, 2 * HD, layer, n_pages)
        + [pl.BlockSpec((None, PAGE, LANES), lambda bb, s, pt: (bb, 0, 0))],
        out_specs=pl.BlockSpec((dec, C_HEADS * HD), lambda bb, s, pt: (bb, 0)),
        scratch_shapes=[pltpu.VMEM((C_HEADS * dec, 1), F32), pltpu.VMEM((C_HEADS * dec, 1), F32),
                        pltpu.VMEM((C_HEADS * dec, LANES), F32),
                        pltpu.VMEM((npg * PAGE, LANES), MXU_DTYPE)])
    return pl.pallas_call(
        functools.partial(_nsa_slc_kernel, npg),
        out_shape=jax.ShapeDtypeStruct((b * dec, C_HEADS * HD), F32),
        grid_spec=grid_spec,
        compiler_params=_cparams(2),
        name="nsa_slc",
    )(pt, cqr, misc, ocw, selsteps, *([pool] * npg), newb)


def _new_pages(rows, b, dec):
    r = rows.reshape(b, dec, rows.shape[-1])
    return _mx(jnp.pad(r, ((0, 0), (0, PAGE - dec), (0, 0))))


def _layer_weights(l, p):
    wuq, wuk, wuv = _layout_mla(p['b_w_uq'][l], p['b_w_uk'][l], p['b_w_uv'][l])
    wout = _mx(p['w_out'][l])
    na, nb = A_HEADS * HD, B_HEADS * D_V
    zero = jnp.zeros((HD, HD), F32)
    return dict(
        w_in=_layout_w_in(p['w_in'][l]), wuq=wuq, wuk=wuk, wuv=wuv,
        wout=(wout[:na], wout[na:na + nb], wout[na + nb:]),
        wp=jnp.concatenate([p['c_cmp_pos_k'][l], p['c_cmp_pos_v'][l]], axis=1),
        wproj=_mx(jnp.block([[p['c_cmp_proj_k'][l], zero], [zero, p['c_cmp_proj_v'][l]]])),
        pre13=_mx(p['ffn_pre_w13'][l]), pre2=_mx(p['ffn_pre_w2'][l]),
        post13=_mx(p['ffn_post_w13'][l]), post2=_mx(p['ffn_post_w2'][l]))


def _prompt_mixers(proj, lw, b, t):
    (aq, akv, akvb, aiq, misc, aikb, qcat, bckv, kb, cq, cqr, ccmp, cslc, cslcb, cwin, cwinb) = proj
    o_a = _dsa_prompt(aiq, misc, aq, aikb, akvb, b, t)
    o_b = _mla_prompt(qcat, kb, lw['wuv'], b, t)
    n_chunk = t // CMP_STRIDE
    n_cmp = n_chunk - CMP_LEN // CMP_STRIDE + 1
    psum = _cmp_rows(ccmp, lw['wp'], b, t).reshape(b, n_chunk, 2 * LANES)
    kvc = _cmp_finish(psum, lw['wproj'], n_cmp)
    o_c = _nsa_prompt(cq, cqr, misc, kvc, cslcb, cwinb, b, t)
    nw = min(WINDOW, t)
    rows = (akv.reshape(b, t, -1), misc.reshape(b, t, -1)[..., :IDX_DIM],
            bckv.reshape(b, t, -1)[..., :KV_LORA + D_ROPE], ccmp.reshape(b, t, -1),
            cslc.reshape(b, t, -1), cwin.reshape(b, t, -1)[:, t - nw:])
    return (o_a, o_b, o_c), rows


def _sample_mixers(proj, lw, l, b, dec, pools, state_win, pt, n_pages):
    (aq, akv, akvb, aiq, misc, aikb, qcat, bckv, kb, cq, cqr, ccmp, cslc, cslcb, cwin, cwinb) = proj
    npg = min(PAGES_PER_STEP, n_pages)
    scp, scn = _dsa_scores(aiq, misc, jnp.swapaxes(pools['a_kidx'], 2, 3), l, pt, _new_pages(aikb, b, dec),
                           b, dec, n_pages)
    o_a = _dsa_sample(aq, scp, scn, pools['a_kv'], l, pt, _new_pages(akvb, b, dec), b, dec, n_pages)
    o_b = _mla_sample(qcat, jnp.swapaxes(pools['b_ckv'], 2, 3), l, pt, _new_pages(kb, b, dec), lw['wuv'],
                      b, dec, n_pages)
    total = n_pages * PAGE + dec
    n_cmp = total // CMP_STRIDE - CMP_LEN // CMP_STRIDE + 1
    psum = _cmp_pages(pools['c_cmp'], l, pt, lw['wp'], b, n_pages)
    kvc = _cmp_finish(psum, lw['wproj'], n_cmp)
    ocw, selsteps, wstate = _nsa_select(cq, cqr, kvc, state_win, l, cwin, _new_pages(cwinb, b, dec),
                                        b, dec, n_pages, npg)
    o_c = _nsa_slc(cqr, misc, ocw, selsteps, pools['c_slc'], l, pt, _new_pages(cslcb, b, dec),
                   b, dec, n_pages, npg)
    rows = (akv.reshape(b, dec, -1), misc.reshape(b, dec, -1)[..., :IDX_DIM],
            bckv.reshape(b, dec, -1)[..., :KV_LORA + D_ROPE], ccmp.reshape(b, dec, -1),
            cslc.reshape(b, dec, -1), wstate)
    return (o_a, o_b, o_c), rows


def _trunk(x, tab, qdt, mixers, weights, p):
    b, t, d = x.shape
    x = x.reshape(b * t, d)
    depth = len(weights)
    per_layer = []
    for l, lw in enumerate(weights):
        x = _ffn(x, p['norm_ffn_pre'][l], lw['pre13'], lw['pre2'])
        proj = _inproj(x, p['norm_mix'][l], lw['w_in'], tab, p['b_q_norm'][l], lw['wuq'], lw['wuk'],
                       p['b_kv_norm'][l], qdt)
        o, rows = mixers(proj, lw, l)
        x = _ffn(x, p['norm_ffn_post'][l], lw['post13'], lw['post2'], mix=tuple(zip(o, lw['wout'])),
                 final_g=p['norm_final'] if l == depth - 1 else None)
        per_layer.append(rows)
    stacked = tuple(jnp.stack(r, axis=0) for r in zip(*per_layer))
    return x.reshape(b, t, d), stacked


def kernel(x_prompt, x_sample, cache_a_kv, cache_a_kidx, cache_b_ckv, cache_c_cmp, cache_c_slc,
           state_c_win, page_table, w_in, w_out, b_w_uq, b_w_uk, b_w_uv, b_q_norm, b_kv_norm,
           c_cmp_pos_k, c_cmp_pos_v, c_cmp_proj_k, c_cmp_proj_v,
           ffn_pre_w13, ffn_pre_w2, ffn_post_w13, ffn_post_w2,
           norm_ffn_pre, norm_mix, norm_ffn_post, norm_final):
    p = {'w_in': w_in, 'w_out': w_out, 'b_w_uq': b_w_uq, 'b_w_uk': b_w_uk, 'b_w_uv': b_w_uv,
         'b_q_norm': b_q_norm, 'b_kv_norm': b_kv_norm,
         'c_cmp_pos_k': c_cmp_pos_k, 'c_cmp_pos_v': c_cmp_pos_v,
         'c_cmp_proj_k': c_cmp_proj_k, 'c_cmp_proj_v': c_cmp_proj_v,
         'ffn_pre_w13': ffn_pre_w13, 'ffn_pre_w2': ffn_pre_w2,
         'ffn_post_w13': ffn_post_w13, 'ffn_post_w2': ffn_post_w2,
         'norm_ffn_pre': norm_ffn_pre, 'norm_mix': norm_mix, 'norm_ffn_post': norm_ffn_post,
         'norm_final': norm_final}
    pools = {'a_kv': cache_a_kv, 'a_kidx': cache_a_kidx, 'b_ckv': cache_b_ckv,
             'c_cmp': cache_c_cmp, 'c_slc': cache_c_slc}
    depth = w_in.shape[0]
    weights = [_layer_weights(l, p) for l in range(depth)]
    bp, tp, _ = x_prompt.shape
    bs, dec, _ = x_sample.shape
    n_pages = page_table.shape[1]
    past_len = n_pages * PAGE
    pt = page_table.reshape(-1).astype(I32)

    tab_p = _rope_table(jnp.arange(tp, dtype=I32))
    tab_s = _rope_table(jnp.tile(past_len + jnp.arange(dec, dtype=I32), bs))

    y_p, rows_p = _trunk(x_prompt, tab_p, MXU_DTYPE,
                         lambda proj, lw, l: _prompt_mixers(proj, lw, bp, tp), weights, p)
    y_s, rows_s = _trunk(x_sample, tab_s, F32,
                         lambda proj, lw, l: _sample_mixers(proj, lw, l, bs, dec, pools, state_c_win, pt, n_pages),
                         weights, p)
    out = [y_p, y_s]
    for rp, rs in zip(rows_p, rows_s):
        out += [rp, rs]
    return tuple(out)
```

```python
import functools

import numpy as np
import jax
import jax.numpy as jnp
from jax import lax
from jax.experimental import pallas as pl
from jax.experimental.pallas import tpu as pltpu

HD = 64
A_HEADS = 4
IDX_HEADS = 8
IDX_DIM = 64
DSA_TOPK = 256
B_HEADS = 8
Q_LORA = 256
KV_LORA = 128
D_NOPE = 64
D_ROPE = 32
D_V = 64
C_HEADS = 4
CMP_LEN = 32
CMP_STRIDE = 16
SLC_BLOCK = 64
N_SLC = 16
WINDOW = 512
N_GATES = 3
ROPE_THETA = 10000.0
NORM_EPS = 1e-6
QBLOCK = 128
FORCE_SCORE = 1e9
PAGE = 128

LANES = 128
MXU_DTYPE = jnp.bfloat16
F32 = jnp.float32
I32 = jnp.int32
NEG = -1e30
INT_MIN = -2147483648
VMEM_LIMIT = 56 * 1024 * 1024
PAGES_PER_STEP = 64
KCHUNK = 512

MISC_W = IDX_DIM
MISC_G = IDX_DIM + IDX_HEADS


def _dot(a, b):
    return jnp.dot(a, b, preferred_element_type=F32)


def _dot_nt(a, b):
    return lax.dot_general(a, b, (((1,), (1,)), ((), ())), preferred_element_type=F32)


def _mx(x):
    return x.astype(MXU_DTYPE)


def _rms(x, g):
    r = lax.rsqrt(jnp.mean(x * x, axis=-1, keepdims=True) + NORM_EPS)
    return x * r * g


def _cparams(n_grid):
    return pltpu.CompilerParams(dimension_semantics=("arbitrary",) * n_grid,
                                vmem_limit_bytes=VMEM_LIMIT)


def _lane(shape):
    return lax.broadcasted_iota(I32, shape, len(shape) - 1)


def _row(shape):
    return lax.broadcasted_iota(I32, shape, len(shape) - 2)


def _order_key(score):
    score = jnp.where(score == 0.0, 0.0, score)
    bits = lax.bitcast_convert_type(score, I32)
    return bits ^ ((bits >> 31) & 0x7FFFFFFF)


def _kth_largest(count_ge, k, shape):
    def body(it, t):
        cand = t ^ jnp.left_shift(jnp.int32(1), 31 - it)
        return jnp.where(count_ge(cand) >= k, cand, t)
    return lax.fori_loop(0, 32, body, jnp.full(shape, INT_MIN, I32))


def _tie_cut(count_tie_below, need, nbits, shape):
    def body(it, cut):
        cand = cut | jnp.left_shift(jnp.int32(1), nbits - 1 - it)
        return jnp.where(count_tie_below(cand) <= need, cand, cut)
    return lax.fori_loop(0, nbits, body, jnp.zeros(shape, I32))


def _lane_tree_sum(x):
    while x.shape[-1] > LANES and (x.shape[-1] // LANES) % 2 == 0:
        half = x.shape[-1] // 2
        x = x[:, :half] + x[:, half:]
    return jnp.sum(x, axis=-1, keepdims=True)


def _topk_mask(key, idx, k, nbits, axis=-1):
    rows = (key.shape[0], 1) if axis == -1 else (1, key.shape[1])

    def count(pred):
        return jnp.sum(jnp.where(pred, 1, 0), axis=axis, keepdims=True)

    thr = jnp.maximum(_kth_largest(lambda cand: count(key >= cand), k, rows), INT_MIN + 1)
    above = key > thr
    tied = key == thr
    need = k - count(above)
    cut = _tie_cut(lambda cand: count(tied & (idx < cand)), need, nbits, rows)
    return above | (tied & (idx < cut))


def _softmax_update(s, mask, m_ref, l_ref, acc_ref, v, v_transposed=False):
    m_old = m_ref[...]
    m_new = jnp.maximum(m_old, jnp.max(s, axis=-1, keepdims=True))
    p = jnp.exp(s - m_new)
    if mask is not None:
        p = jnp.where(mask, p, 0.0)
    alpha = jnp.exp(m_old - m_new)
    l_ref[...] = alpha * l_ref[...] + jnp.sum(p, axis=-1, keepdims=True)
    pv = _dot_nt(_mx(p), v) if v_transposed else _dot(_mx(p), v)
    acc_ref[...] = alpha * acc_ref[...] + pv
    m_ref[...] = m_new


def _flash_chunk_t(q, k, vt, mask, m_ref, l_ref, acc_ref):
    s = _dot_nt(k, q)
    if mask is not None:
        s = jnp.where(mask, s, NEG)
    m_old = m_ref[...]
    m_new = jnp.maximum(m_old, jnp.max(s, axis=0, keepdims=True))
    p = jnp.exp(s - m_new)
    if mask is not None:
        p = jnp.where(mask, p, 0.0)
    alpha = jnp.exp(m_old - m_new)
    l_ref[...] = alpha * l_ref[...] + jnp.sum(p, axis=0, keepdims=True)
    acc_ref[...] = alpha * acc_ref[...] + _dot(vt, _mx(p))
    m_ref[...] = m_new


def _chunked_transpose(x, b, t, kc):
    return jnp.swapaxes(x.reshape(b, t // kc, kc, x.shape[-1]), 2, 3)


def _tile_rows(x, n):
    return jnp.concatenate([x] * n, axis=0)


def _ffn_kernel(n_mix, has_final, *refs):
    x_ref = refs[0]
    mix = refs[1:1 + 2 * n_mix]
    g_ref, w1_ref, w3_ref, w2_ref = refs[1 + 2 * n_mix:5 + 2 * n_mix]
    pos = 5 + 2 * n_mix
    gf_ref = refs[pos] if has_final else None
    pos += 1 if has_final else 0
    out_ref, xs, hs, acc = refs[pos:pos + 4]
    f = pl.program_id(1)

    @pl.when(f == 0)
    def _():
        x = x_ref[...]
        for i in range(n_mix):
            x = x + _dot(_mx(mix[2 * i][...]), mix[2 * i + 1][...])
        xs[...] = x
        hs[...] = _mx(_rms(x, g_ref[...]))
        acc[...] = jnp.zeros_like(acc)

    h = hs[...]
    gate = _dot(h, w1_ref[...])
    up = _dot(h, w3_ref[...])
    acc[...] += _dot(_mx(jax.nn.silu(gate) * up), w2_ref[...])

    @pl.when(f == pl.num_programs(1) - 1)
    def _():
        y = xs[...] + 0.5 * acc[...]
        if has_final:
            y = _rms(y, gf_ref[...])
        out_ref[...] = y


def _ffn(x, g, w13, w2, mix=(), final_g=None):
    n, d = x.shape
    d_ff = w2.shape[0]
    tm = min(1024, n)
    tf = 256
    nf = d_ff // tf
    assert n % tm == 0 and d_ff % tf == 0
    in_specs = [pl.BlockSpec((tm, d), lambda i, f: (i, 0))]
    args = [x]
    for o, w in mix:
        in_specs += [pl.BlockSpec((tm, o.shape[1]), lambda i, f: (i, 0)),
                     pl.BlockSpec(w.shape, lambda i, f: (0, 0))]
        args += [o, w]
    in_specs += [pl.BlockSpec((1, d), lambda i, f: (0, 0)),
                 pl.BlockSpec((d, tf), lambda i, f: (0, f)),
                 pl.BlockSpec((d, tf), lambda i, f: (0, f + nf)),
                 pl.BlockSpec((tf, d), lambda i, f: (f, 0))]
    args += [g.reshape(1, d), w13, w13, w2]
    if final_g is not None:
        in_specs.append(pl.BlockSpec((1, d), lambda i, f: (0, 0)))
        args.append(final_g.reshape(1, d))
    return pl.pallas_call(
        functools.partial(_ffn_kernel, len(mix), final_g is not None),
        out_shape=jax.ShapeDtypeStruct((n, d), F32),
        grid=(n // tm, nf),
        in_specs=in_specs,
        out_specs=pl.BlockSpec((tm, d), lambda i, f: (i, 0)),
        scratch_shapes=[pltpu.VMEM((tm, d), F32), pltpu.VMEM((tm, d), MXU_DTYPE),
                        pltpu.VMEM((tm, d), F32)],
        compiler_params=_cparams(2),
        name="ffn",
    )(*args)


C_AQ, C_AKV, C_AIQ, C_MISC, C_BCQ, C_BCKV, C_BKR, C_CQ, C_CCMP, C_CSLC, C_CWIN, C_END = (
    0, 256, 384, 896, 1024, 1280, 1408, 1536, 1792, 1920, 2048, 2176)
T_HH, T_HV, T_R1 = 0, 2, 4


def _rope(x, tab_ref, kind, half):
    cos = tab_ref[:, kind * LANES:(kind + 1) * LANES]
    ssin = tab_ref[:, (kind + 1) * LANES:(kind + 2) * LANES]
    first = (_lane(x.shape) % (2 * half)) < half
    rot = jnp.where(first, pltpu.roll(x, LANES - half, 1), pltpu.roll(x, half, 1))
    return x * cos + rot * ssin


def _split_heads(r):
    low = _lane(r.shape) < HD
    return jnp.where(low, r, 0.0), jnp.where(low, pltpu.roll(r, HD, 1), 0.0)


def _inproj_kernel(x_ref, g_ref, w_ref, tab_ref, qn_ref, wuq_ref, wuk_ref, kvn_ref,
                   aq_ref, akv_ref, akvb_ref, aiq_ref, misc_ref, aikb_ref, qcat_ref,
                   bckv_ref, kb_ref, cq_ref, cqr_ref, ccmp_ref, cslc_ref, cslcb_ref,
                   cwin_ref, cwinb_ref):
    h = _mx(_rms(x_ref[...], g_ref[...]))

    def proj(c0, c1):
        return _dot(h, w_ref[:, c0:c1])

    qscale = HD ** -0.5
    for i in range(A_HEADS // 2):
        r = _rope(proj(C_AQ + LANES * i, C_AQ + LANES * (i + 1)), tab_ref, T_HH, HD // 2) * qscale
        h0, h1 = _split_heads(r)
        aq_ref[2 * i] = h0.astype(aq_ref.dtype)
        aq_ref[2 * i + 1] = h1.astype(aq_ref.dtype)

    akv = _rope(proj(C_AKV, C_AIQ), tab_ref, T_HV, HD // 2)
    akv_ref[...] = akv
    akvb_ref[...] = _mx(akv)

    iscale = IDX_DIM ** -0.5
    for i in range(IDX_HEADS // 2):
        r = _rope(proj(C_AIQ + LANES * i, C_AIQ + LANES * (i + 1)), tab_ref, T_HH, IDX_DIM // 2) * iscale
        h0, h1 = _split_heads(r)
        aiq_ref[2 * i] = h0.astype(aiq_ref.dtype)
        aiq_ref[2 * i + 1] = h1.astype(aiq_ref.dtype)

    misc = _rope(proj(C_MISC, C_BCQ), tab_ref, T_HV, IDX_DIM // 2)
    lane = _lane(misc.shape)
    misc = jnp.where((lane >= MISC_W) & (lane < MISC_G), misc * IDX_HEADS ** -0.5, misc)
    misc = jnp.where((lane >= MISC_G) & (lane < MISC_G + C_HEADS * N_GATES),
                     jax.nn.sigmoid(misc), misc)
    misc_ref[...] = misc
    aikb_ref[...] = _mx(jnp.where(lane < IDX_DIM, misc, 0.0))

    bscale = (D_NOPE + D_ROPE) ** -0.5
    cqn = _mx(_rms(proj(C_BCQ, C_BCKV), qn_ref[...]))
    for p in range(B_HEADS // 2):
        nope = _mx(_dot(cqn, wuq_ref[:, LANES * p:LANES * (p + 1)]))
        lat = _dot(nope, wuk_ref[p]) * bscale
        qcat_ref[2 * p, :, 0:LANES] = lat[:, :LANES].astype(qcat_ref.dtype)
        qcat_ref[2 * p + 1, :, 0:LANES] = lat[:, LANES:].astype(qcat_ref.dtype)
    for hh in range(B_HEADS):
        c0 = B_HEADS * D_NOPE + LANES * hh
        qr = _rope(_dot(cqn, wuq_ref[:, c0:c0 + LANES]), tab_ref, T_R1, D_ROPE // 2) * bscale
        qcat_ref[hh, :, LANES:2 * LANES] = qr.astype(qcat_ref.dtype)

    lat = _rms(proj(C_BCKV, C_BKR), kvn_ref[...])
    kr = _rope(proj(C_BKR, C_CQ), tab_ref, T_R1, D_ROPE // 2)
    bckv_ref[:, 0:LANES] = lat
    bckv_ref[:, LANES:2 * LANES] = kr
    kb_ref[:, 0:LANES] = _mx(lat)
    kb_ref[:, LANES:2 * LANES] = _mx(kr)

    for i in range(C_HEADS // 2):
        z = proj(C_CQ + LANES * i, C_CQ + LANES * (i + 1)) * qscale
        h0, h1 = _split_heads(z)
        cq_ref[2 * i] = h0.astype(cq_ref.dtype)
        cq_ref[2 * i + 1] = h1.astype(cq_ref.dtype)
        h0, h1 = _split_heads(_rope(z, tab_ref, T_HH, HD // 2))
        cqr_ref[2 * i] = h0.astype(cqr_ref.dtype)
        cqr_ref[2 * i + 1] = h1.astype(cqr_ref.dtype)

    ccmp_ref[...] = proj(C_CCMP, C_CSLC)
    cslc = _rope(proj(C_CSLC, C_CWIN), tab_ref, T_HV, HD // 2)
    cslc_ref[...] = cslc
    cslcb_ref[...] = _mx(cslc)
    cwin = _rope(proj(C_CWIN, C_END), tab_ref, T_HV, HD // 2)
    cwin_ref[...] = cwin
    cwinb_ref[...] = _mx(cwin)


def _inproj(x, g, w, tab, qn, wuq, wuk, kvn, qdt):
    n, d = x.shape
    tm = min(1024, n)
    nt = tab.shape[0] // tm
    assert n % tm == 0 and tab.shape[0] % tm == 0
    row = lambda i: (i, 0)
    hrow = lambda i: (0, i, 0)
    const2 = lambda i: (0, 0)
    const3 = lambda i: (0, 0, 0)
    sds = jax.ShapeDtypeStruct
    out_shape = (
        sds((A_HEADS, n, LANES), qdt), sds((n, LANES), F32), sds((n, LANES), MXU_DTYPE),
        sds((IDX_HEADS, n, LANES), qdt), sds((n, LANES), F32), sds((n, LANES), MXU_DTYPE),
        sds((B_HEADS, n, 2 * LANES), qdt), sds((n, 2 * LANES), F32), sds((n, 2 * LANES), MXU_DTYPE),
        sds((C_HEADS, n, LANES), qdt), sds((C_HEADS, n, LANES), qdt),
        sds((n, LANES), F32), sds((n, LANES), F32), sds((n, LANES), MXU_DTYPE),
        sds((n, LANES), F32), sds((n, LANES), MXU_DTYPE))
    out_specs = (
        pl.BlockSpec((A_HEADS, tm, LANES), hrow), pl.BlockSpec((tm, LANES), row), pl.BlockSpec((tm, LANES), row),
        pl.BlockSpec((IDX_HEADS, tm, LANES), hrow), pl.BlockSpec((tm, LANES), row), pl.BlockSpec((tm, LANES), row),
        pl.BlockSpec((B_HEADS, tm, 2 * LANES), hrow), pl.BlockSpec((tm, 2 * LANES), row),
        pl.BlockSpec((tm, 2 * LANES), row),
        pl.BlockSpec((C_HEADS, tm, LANES), hrow), pl.BlockSpec((C_HEADS, tm, LANES), hrow),
        pl.BlockSpec((tm, LANES), row), pl.BlockSpec((tm, LANES), row), pl.BlockSpec((tm, LANES), row),
        pl.BlockSpec((tm, LANES), row), pl.BlockSpec((tm, LANES), row))
    return pl.pallas_call(
        _inproj_kernel,
        out_shape=out_shape,
        grid=(n // tm,),
        in_specs=[pl.BlockSpec((tm, d), row), pl.BlockSpec((1, d), const2),
                  pl.BlockSpec(w.shape, const2),
                  pl.BlockSpec((tm, tab.shape[1]), lambda i: (i % nt, 0)),
                  pl.BlockSpec((1, Q_LORA), const2), pl.BlockSpec(wuq.shape, const2),
                  pl.BlockSpec(wuk.shape, const3), pl.BlockSpec((1, KV_LORA), const2)],
        out_specs=out_specs,
        compiler_params=_cparams(1),
        name="inproj",
    )(x, g.reshape(1, d), w, tab, qn.reshape(1, Q_LORA), wuq, wuk, kvn.reshape(1, KV_LORA))


def _rope_table(pos):
    pos = pos.astype(F32)[:, None]

    def cs(half):
        inv = ROPE_THETA ** (-jnp.arange(half, dtype=F32) / half)
        ang = pos * inv[None, :]
        return jnp.cos(ang), jnp.sin(ang)

    n = pos.shape[0]
    c32, s32 = cs(HD // 2)
    c16, s16 = cs(D_ROPE // 2)
    one = lambda w: jnp.ones((n, w), F32)
    zero = lambda w: jnp.zeros((n, w), F32)
    cols = [jnp.concatenate([c32, c32, c32, c32], 1), jnp.concatenate([-s32, s32, -s32, s32], 1),
            jnp.concatenate([c32, c32, one(HD)], 1), jnp.concatenate([-s32, s32, zero(HD)], 1),
            jnp.concatenate([c16, c16, one(LANES - D_ROPE)], 1),
            jnp.concatenate([-s16, s16, zero(LANES - D_ROPE)], 1)]
    return jnp.concatenate(cols, axis=1)


def _layout_w_in(w):
    sizes = (A_HEADS * HD, HD, HD, IDX_HEADS * IDX_DIM, IDX_HEADS, IDX_DIM, Q_LORA, KV_LORA, D_ROPE,
             C_HEADS * HD, HD, HD, HD, HD, HD, HD, C_HEADS * N_GATES)
    offs = np.concatenate([[0], np.cumsum(sizes)])
    (a_q, a_k, a_v, a_iq, a_iw, a_ik, b_cq, b_ckv, b_kr,
     c_q, c_kc, c_vc, c_ks, c_vs, c_kw, c_vw, c_g) = [w[:, offs[i]:offs[i + 1]] for i in range(len(sizes))]
    z = lambda k: jnp.zeros((w.shape[0], k), w.dtype)
    misc_pad = LANES - IDX_DIM - IDX_HEADS - C_HEADS * N_GATES
    out = jnp.concatenate([a_q, a_k, a_v, a_iq, a_ik, a_iw, c_g, z(misc_pad), b_cq, b_ckv,
                           b_kr, z(LANES - D_ROPE), c_q, c_kc, c_vc, c_ks, c_vs, c_kw, c_vw], axis=1)
    assert out.shape[1] == C_END
    return _mx(out)


def _layout_mla(w_uq, w_uk, w_uv):
    d_qk = D_NOPE + D_ROPE
    uq = w_uq.reshape(Q_LORA, B_HEADS, d_qk)
    nope = uq[:, :, :D_NOPE].reshape(Q_LORA, B_HEADS * D_NOPE)
    rope = jnp.pad(uq[:, :, D_NOPE:], ((0, 0), (0, 0), (0, LANES - D_ROPE))).reshape(Q_LORA, B_HEADS * LANES)
    wuq = _mx(jnp.concatenate([nope, rope], axis=1))
    ukt = jnp.transpose(w_uk, (1, 2, 0))
    uvh = jnp.transpose(w_uv, (1, 0, 2))
    zk = jnp.zeros((D_NOPE, KV_LORA), w_uk.dtype)
    zv = jnp.zeros((KV_LORA, D_V), w_uv.dtype)
    wuk = jnp.stack([jnp.block([[ukt[2 * p], zk], [zk, ukt[2 * p + 1]]]) for p in range(B_HEADS // 2)])
    wuv = jnp.stack([jnp.block([[uvh[2 * p], zv], [zv, uvh[2 * p + 1]]]) for p in range(B_HEADS // 2)])
    return wuq, _mx(wuk), _mx(wuv)


def _dsa_prompt_kernel(n_keep, qi_ref, misc_ref, aq_ref, kidx_ref, akv_ref, akvt_ref, o_ref,
                       keys_ref, m_ref, l_ref, acc_ref):
    i = pl.program_id(1)
    tq = QBLOCK
    kc = keys_ref.shape[1]
    nch = (i * tq) // kc + 1
    qi = qi_ref[...].reshape(IDX_HEADS * tq, LANES)
    misc_t = misc_ref[...].T
    kpos0 = _row((kc, tq))
    qpos = i * tq + _lane((kc, tq))

    def score_chunk(c, carry):
        k = kidx_ref[pl.ds(pl.multiple_of(c * kc, kc), kc), :]
        rel = jnp.maximum(_dot_nt(k, qi), 0.0)
        score = jnp.zeros((kc, tq), F32)
        for hh in range(IDX_HEADS):
            score = score + rel[:, hh * tq:(hh + 1) * tq] * misc_t[MISC_W + hh:MISC_W + hh + 1, :]
        keys_ref[c] = jnp.where((c * kc + kpos0) <= qpos, _order_key(score), INT_MIN)
        return carry

    lax.fori_loop(0, nch, score_chunk, 0)

    def count_where(pred_fn):
        def body(c, cnt):
            hit = jnp.where(pred_fn(c, keys_ref[c]), 1, 0)
            return cnt + jnp.sum(hit.reshape(kc // 8, 8, tq), axis=0)
        cnt = lax.fori_loop(0, nch, body, jnp.zeros((8, tq), I32))
        return jnp.sum(cnt, axis=0, keepdims=True)

    def count_ge(cand):
        return count_where(lambda c, kk: kk >= cand)

    thr = jnp.maximum(_kth_largest(count_ge, n_keep, (1, tq)), INT_MIN + 1)

    @pl.when(jnp.max(count_ge(thr)) > n_keep)
    def _():
        need = n_keep - count_ge(thr + 1)
        cut = _tie_cut(lambda cand: count_where(lambda c, kk: (kk == thr) & ((c * kc + kpos0) < cand)),
                       need, int(kidx_ref.shape[0]).bit_length(), (1, tq))

        def drop(c, carry):
            kk = keys_ref[c]
            keys_ref[c] = jnp.where((kk == thr) & ((c * kc + kpos0) >= cut), INT_MIN, kk)
            return carry

        lax.fori_loop(0, nch, drop, 0)

    m_ref[...] = jnp.full_like(m_ref, NEG)
    l_ref[...] = jnp.zeros_like(l_ref)
    acc_ref[...] = jnp.zeros_like(acc_ref)
    qa = aq_ref[...].reshape(A_HEADS * tq, LANES)

    def attn_chunk(c, carry):
        kv = akv_ref[pl.ds(pl.multiple_of(c * kc, kc), kc), :]
        mask = jnp.concatenate([keys_ref[c] >= thr] * A_HEADS, axis=1)
        _flash_chunk_t(qa, kv, akvt_ref[c], mask, m_ref, l_ref, acc_ref)
        return carry

    lax.fori_loop(0, nch, attn_chunk, 0)
    ot = acc_ref[...] / l_ref[...]
    for hh in range(A_HEADS):
        o_ref[:, hh * HD:(hh + 1) * HD] = ot[:, hh * tq:(hh + 1) * tq].T[:, HD:].astype(o_ref.dtype)


def _dsa_prompt(aiq, misc, aq, aikb, akvb, b, t):
    n = b * t
    nq = t // QBLOCK
    kc = min(KCHUNK, t)
    n_keep = min(DSA_TOPK, t // 4)
    qmap = lambda bb, i: (0, bb * nq + i, 0)
    akvt = _chunked_transpose(akvb, b, t, kc)
    return pl.pallas_call(
        functools.partial(_dsa_prompt_kernel, n_keep),
        out_shape=jax.ShapeDtypeStruct((n, A_HEADS * HD), MXU_DTYPE),
        grid=(b, nq),
        in_specs=[pl.BlockSpec((IDX_HEADS, QBLOCK, LANES), qmap),
                  pl.BlockSpec((QBLOCK, LANES), lambda bb, i: (bb * nq + i, 0)),
                  pl.BlockSpec((A_HEADS, QBLOCK, LANES), qmap),
                  pl.BlockSpec((t, LANES), lambda bb, i: (bb, 0)),
                  pl.BlockSpec((t, LANES), lambda bb, i: (bb, 0)),
                  pl.BlockSpec((None, t // kc, LANES, kc), lambda bb, i: (bb, 0, 0, 0))],
        out_specs=pl.BlockSpec((QBLOCK, A_HEADS * HD), lambda bb, i: (bb * nq + i, 0)),
        scratch_shapes=[pltpu.VMEM((t // kc, kc, QBLOCK), I32),
                        pltpu.VMEM((1, A_HEADS * QBLOCK), F32), pltpu.VMEM((1, A_HEADS * QBLOCK), F32),
                        pltpu.VMEM((LANES, A_HEADS * QBLOCK), F32)],
        compiler_params=_cparams(2),
        name="dsa_prompt",
    )(aiq, misc, aq, aikb, akvb, akvt)


def _mla_prompt_kernel(q_ref, kb_ref, latt_ref, wuv_ref, o_ref, m_ref, l_ref, acc_ref):
    i = pl.program_id(1)
    tq = QBLOCK
    kc = latt_ref.shape[2]
    rows = B_HEADS * tq
    q = q_ref[...].reshape(rows, 2 * LANES)
    m_ref[...] = jnp.full_like(m_ref, NEG)
    l_ref[...] = jnp.zeros_like(l_ref)
    acc_ref[...] = jnp.zeros_like(acc_ref)

    def chunk(c, masked):
        k = kb_ref[pl.ds(pl.multiple_of(c * kc, kc), kc), :]
        mask = None
        if masked:
            mask = (c * kc + _row((kc, rows))) <= (i * tq + _lane((kc, rows)) % tq)
        _flash_chunk_t(q, k, latt_ref[c], mask, m_ref, l_ref, acc_ref)

    nfull = (i * tq) // kc

    def body(c, carry):
        chunk(c, False)
        return carry

    lax.fori_loop(0, nfull, body, 0)
    chunk(nfull, True)
    ot = acc_ref[...] / l_ref[...]
    for p in range(B_HEADS // 2):
        pair = jnp.concatenate([ot[:, 2 * p * tq:(2 * p + 1) * tq].T, ot[:, (2 * p + 1) * tq:(2 * p + 2) * tq].T],
                               axis=1)
        o_ref[:, p * LANES:(p + 1) * LANES] = _dot(_mx(pair), wuv_ref[p]).astype(o_ref.dtype)


def _mla_prompt(qcat, kb, wuv, b, t):
    n = b * t
    nq = t // QBLOCK
    kc = min(KCHUNK, t)
    latt = _chunked_transpose(kb[:, :KV_LORA], b, t, kc)
    return pl.pallas_call(
        _mla_prompt_kernel,
        out_shape=jax.ShapeDtypeStruct((n, B_HEADS * D_V), MXU_DTYPE),
        grid=(b, nq),
        in_specs=[pl.BlockSpec((B_HEADS, QBLOCK, 2 * LANES), lambda bb, i: (0, bb * nq + i, 0)),
                  pl.BlockSpec((t, 2 * LANES), lambda bb, i: (bb, 0)),
                  pl.BlockSpec((None, t // kc, KV_LORA, kc), lambda bb, i: (bb, 0, 0, 0)),
                  pl.BlockSpec(wuv.shape, lambda bb, i: (0, 0, 0))],
        out_specs=pl.BlockSpec((QBLOCK, B_HEADS * D_V), lambda bb, i: (bb * nq + i, 0)),
        scratch_shapes=[pltpu.VMEM((1, B_HEADS * QBLOCK), F32), pltpu.VMEM((1, B_HEADS * QBLOCK), F32),
                        pltpu.VMEM((KV_LORA, B_HEADS * QBLOCK), F32)],
        compiler_params=_cparams(2),
        name="mla_prompt",
    )(qcat, kb, latt, wuv)


def _chunk_sums(x, wp):
    x3 = x.reshape(x.shape[0] // CMP_STRIDE, CMP_STRIDE, LANES)
    p0 = jnp.sum(x3 * wp[None, :CMP_STRIDE], axis=1)
    p1 = jnp.sum(x3 * wp[None, CMP_STRIDE:], axis=1)
    return jnp.concatenate([p0, p1], axis=1)


def _cmp_rows_kernel(x_ref, wp_ref, p_ref):
    p_ref[...] = _chunk_sums(x_ref[...], wp_ref[...])


def _cmp_rows(rows, wp, b, t):
    tr = min(1024, t)
    return pl.pallas_call(
        _cmp_rows_kernel,
        out_shape=jax.ShapeDtypeStruct((b * t // CMP_STRIDE, 2 * LANES), F32),
        grid=(b * t // tr,),
        in_specs=[pl.BlockSpec((tr, LANES), lambda i: (i, 0)), pl.BlockSpec(wp.shape, lambda i: (0, 0))],
        out_specs=pl.BlockSpec((tr // CMP_STRIDE, 2 * LANES), lambda i: (i, 0)),
        compiler_params=_cparams(1),
        name="cmp_rows",
    )(rows, wp)


def _cmp_pages_kernel(npg, pt_ref, *refs):
    pages, wp_ref, p_ref = refs[:npg], refs[npg], refs[npg + 1]
    cpp = PAGE // CMP_STRIDE
    for g in range(npg):
        p_ref[g * cpp:(g + 1) * cpp, :] = _chunk_sums(pages[g][...], wp_ref[...])


def _page_specs(npg, width, layer, n_pages, transposed=False):
    block = (None, None, width, PAGE) if transposed else (None, None, PAGE, width)

    def spec(g):
        return pl.BlockSpec(block, lambda bb, s, pt: (layer, pt[bb * n_pages + s * npg + g], 0, 0))
    return [spec(g) for g in range(npg)]


def _cmp_pages(pool, layer, pt, wp, b, n_pages):
    npg = min(PAGES_PER_STEP, n_pages)
    cpp = PAGE // CMP_STRIDE
    grid_spec = pltpu.PrefetchScalarGridSpec(
        num_scalar_prefetch=1, grid=(b, n_pages // npg),
        in_specs=_page_specs(npg, LANES, layer, n_pages) + [pl.BlockSpec(wp.shape, lambda bb, s, pt: (0, 0))],
        out_specs=pl.BlockSpec((None, npg * cpp, 2 * LANES), lambda bb, s, pt: (bb, s, 0)))
    return pl.pallas_call(
        functools.partial(_cmp_pages_kernel, npg),
        out_shape=jax.ShapeDtypeStruct((b, n_pages * cpp, 2 * LANES), F32),
        grid_spec=grid_spec,
        compiler_params=_cparams(2),
        name="cmp_pages",
    )(pt, *([pool] * npg), wp)


def _cmp_finish_kernel(n_cmp, p0_ref, p1_ref, w_ref, o_ref):
    blk = p0_ref[:, :LANES] + p1_ref[...]
    out = _dot(_mx(blk), w_ref[...])
    o_ref[...] = jnp.where(_row(out.shape) < n_cmp, out, 0.0).astype(o_ref.dtype)


def _cmp_finish(p, wproj, n_cmp):
    b, n_chunk, _ = p.shape
    p1s = jnp.pad(p[:, 1:, LANES:], ((0, 0), (0, 1), (0, 0)))
    return pl.pallas_call(
        functools.partial(_cmp_finish_kernel, n_cmp),
        out_shape=jax.ShapeDtypeStruct((b, n_chunk, LANES), MXU_DTYPE),
        grid=(b,),
        in_specs=[pl.BlockSpec((None, n_chunk, 2 * LANES), lambda bb: (bb, 0, 0)),
                  pl.BlockSpec((None, n_chunk, LANES), lambda bb: (bb, 0, 0)),
                  pl.BlockSpec(wproj.shape, lambda bb: (0, 0))],
        out_specs=pl.BlockSpec((None, n_chunk, LANES), lambda bb: (bb, 0, 0)),
        compiler_params=_cparams(1),
        name="cmp_finish",
    )(p, p1s, wproj)


def _band_matrix(n_rows, n_cols):
    r = SLC_BLOCK // CMP_STRIDE
    c = CMP_LEN // CMP_STRIDE
    i = np.arange(n_rows)[:, None]
    j = np.arange(n_cols)[None, :]
    return jnp.asarray((i >= r * j - (c - 1)) & (i <= r * j + r - 1), MXU_DTYPE)


def _importance(pc, band):
    hi = pc.astype(jnp.bfloat16)
    r1 = pc - hi.astype(F32)
    mid = r1.astype(jnp.bfloat16)
    lo = (r1 - mid.astype(F32)).astype(jnp.bfloat16)
    band = band.astype(jnp.bfloat16)
    return _dot(hi, band) + _dot(mid, band) + _dot(lo, band)


def _select_blocks(imp, j, qpos, n_slc, n_top, axis=-1):
    cur = qpos // SLC_BLOCK
    forced = (j == 0) | (j == cur) | (j == cur - 1)
    adm = (j * SLC_BLOCK <= qpos) & (j < n_slc)
    score = jnp.where(forced, FORCE_SCORE, jnp.where(adm, imp, -jnp.inf))
    key = jnp.where(adm, _order_key(score), INT_MIN)
    return jnp.where(_topk_mask(key, j, n_top, int(imp.shape[axis]).bit_length(), axis), 1.0, 0.0)


def _masked_softmax(s, mask):
    s = jnp.where(mask, s, NEG)
    m = jnp.max(s, axis=-1, keepdims=True)
    e = jnp.where(mask, jnp.exp(s - m), 0.0)
    return e / jnp.maximum(jnp.sum(e, axis=-1, keepdims=True), jnp.finfo(F32).tiny)


def _nsa_prompt_kernel(n_cmp, n_slc, n_top, cq_ref, cqr_ref, misc_ref, kvc_ref, band_ref,
                       slc_ref, slct_ref, win_ref, o_ref, m_ref, l_ref, acc_ref):
    i = pl.program_id(1)
    tq = QBLOCK
    t = slc_ref.shape[0]
    kc = min(KCHUNK, t)
    q = cq_ref[...].reshape(C_HEADS * tq, LANES)
    qr = cqr_ref[...].reshape(C_HEADS * tq, LANES)

    kvc = kvc_ref[...]
    ncp = kvc.shape[0]
    qpos_c = i * tq + _row((tq, ncp))
    n_id = _lane((tq, ncp))
    cmask = (n_id * CMP_STRIDE + CMP_LEN - 1 <= qpos_c) & (n_id < n_cmp)
    p_cmp = _masked_softmax(_dot_nt(q, kvc), _tile_rows(cmask, C_HEADS))
    o_cmp = _dot(_mx(p_cmp), kvc)
    pc = p_cmp[0:tq]
    for hh in range(1, C_HEADS):
        pc = pc + p_cmp[hh * tq:(hh + 1) * tq]

    imp = _importance(pc, band_ref[...])
    imp_t = imp.T
    sel_t = _select_blocks(imp_t, _row(imp_t.shape), i * tq + _lane(imp_t.shape), n_slc, n_top, axis=0)
    sel_t = sel_t.astype(jnp.bfloat16)

    m_ref[...] = jnp.full_like(m_ref, NEG)
    l_ref[...] = jnp.zeros_like(l_ref)
    acc_ref[...] = jnp.zeros_like(acc_ref)
    nsl = sel_t.shape[0]
    qpos = i * tq + _lane((kc, tq))

    def slc_chunk(c, carry):
        kv = slc_ref[pl.ds(pl.multiple_of(c * kc, kc), kc), :]
        kblock = (c * kc + _row((kc, nsl))) // SLC_BLOCK
        expand = jnp.where(_lane((kc, nsl)) == kblock, 1.0, 0.0).astype(jnp.bfloat16)
        visible = (c * kc + _row((kc, tq))) <= qpos
        picked = jnp.where(visible, _dot(expand, sel_t), 0.0) > 0.5
        _flash_chunk_t(qr, kv, slct_ref[c], jnp.concatenate([picked] * C_HEADS, axis=1), m_ref, l_ref, acc_ref)
        return carry

    lax.fori_loop(0, (i * tq) // kc + 1, slc_chunk, 0)
    ot = acc_ref[...] / jnp.maximum(l_ref[...], jnp.finfo(F32).tiny)
    o_slc = jnp.concatenate([ot[:, hh * tq:(hh + 1) * tq].T for hh in range(C_HEADS)], axis=0)

    nw = min(WINDOW + tq, t)
    start = jnp.maximum(i * tq + tq - nw, 0)
    rows = win_ref[pl.ds(pl.multiple_of(start, tq), nw), :]
    dist = (i * tq + _row((tq, nw))) - (start + _lane((tq, nw)))
    wmask = _tile_rows((dist >= 0) & (dist < WINDOW), C_HEADS)
    o_win = _dot(_mx(_masked_softmax(_dot_nt(qr, rows), wmask)), rows)

    misc = misc_ref[...]
    for hh in range(C_HEADS):
        g = [misc[:, MISC_G + N_GATES * hh + k:MISC_G + N_GATES * hh + k + 1] for k in range(N_GATES)]
        sl = slice(hh * tq, (hh + 1) * tq)
        o = g[0] * o_cmp[sl] + g[1] * o_slc[sl] + g[2] * o_win[sl]
        o_ref[:, hh * HD:(hh + 1) * HD] = o[:, HD:].astype(o_ref.dtype)


def _nsa_prompt(cq, cqr, misc, kvc, slcb, winb, b, t):
    n = b * t
    nq = t // QBLOCK
    n_chunk = t // CMP_STRIDE
    n_cmp = n_chunk - CMP_LEN // CMP_STRIDE + 1
    n_slc = -(-t // SLC_BLOCK)
    n_top = min(N_SLC, n_slc)
    nsl = -(-n_slc // LANES) * LANES
    band = _band_matrix(n_chunk, nsl)
    kc = min(KCHUNK, t)
    qmap = lambda bb, i: (0, bb * nq + i, 0)
    return pl.pallas_call(
        functools.partial(_nsa_prompt_kernel, n_cmp, n_slc, n_top),
        out_shape=jax.ShapeDtypeStruct((n, C_HEADS * HD), MXU_DTYPE),
        grid=(b, nq),
        in_specs=[pl.BlockSpec((C_HEADS, QBLOCK, LANES), qmap),
                  pl.BlockSpec((C_HEADS, QBLOCK, LANES), qmap),
                  pl.BlockSpec((QBLOCK, LANES), lambda bb, i: (bb * nq + i, 0)),
                  pl.BlockSpec((None, n_chunk, LANES), lambda bb, i: (bb, 0, 0)),
                  pl.BlockSpec(band.shape, lambda bb, i: (0, 0)),
                  pl.BlockSpec((t, LANES), lambda bb, i: (bb, 0)),
                  pl.BlockSpec((None, t // kc, LANES, kc), lambda bb, i: (bb, 0, 0, 0)),
                  pl.BlockSpec((t, LANES), lambda bb, i: (bb, 0))],
        out_specs=pl.BlockSpec((QBLOCK, C_HEADS * HD), lambda bb, i: (bb * nq + i, 0)),
        scratch_shapes=[pltpu.VMEM((1, C_HEADS * QBLOCK), F32), pltpu.VMEM((1, C_HEADS * QBLOCK), F32),
                        pltpu.VMEM((LANES, C_HEADS * QBLOCK), F32)],
        compiler_params=_cparams(2),
        name="nsa_prompt",
    )(cq, cqr, misc, kvc, band, slcb, _chunked_transpose(slcb, b, t, kc), winb)


def _stack_heads(ref):
    x = ref[...]
    return _mx(x.reshape(x.shape[0] * x.shape[1], x.shape[2]))


def _new_mask(shape, dec):
    return (_lane(shape) <= _row(shape) % dec) & (_lane(shape) < dec)


def _dsa_scores_kernel(npg, pt_ref, *refs):
    qi_ref, misc_ref = refs[0], refs[1]
    pages = refs[2:2 + npg]
    newk_ref, scp_ref, scn_ref, kbuf = refs[2 + npg:6 + npg]
    dec = misc_ref.shape[0]
    q = _stack_heads(qi_ref)[:, :IDX_DIM]
    misc = misc_ref[...]

    def weighted(rel):
        score = jnp.zeros((dec, rel.shape[1]), F32)
        for hh in range(IDX_HEADS):
            score = score + rel[hh * dec:(hh + 1) * dec] * misc[:, MISC_W + hh:MISC_W + hh + 1]
        return score

    for g in range(npg):
        kbuf[:, g * PAGE:(g + 1) * PAGE] = _mx(pages[g][...])
    scp_ref[...] = weighted(jnp.maximum(_dot(q, kbuf[...]), 0.0))

    @pl.when(pl.program_id(1) == 0)
    def _():
        sc = weighted(jnp.maximum(_dot_nt(q, newk_ref[:, :IDX_DIM]), 0.0))
        scn_ref[...] = jnp.where(_new_mask(sc.shape, dec), sc, -jnp.inf)


def _dsa_scores(aiq, misc, pool, layer, pt, newk, b, dec, n_pages):
    npg = min(PAGES_PER_STEP, n_pages)
    grid_spec = pltpu.PrefetchScalarGridSpec(
        num_scalar_prefetch=1, grid=(b, n_pages // npg),
        in_specs=[pl.BlockSpec((IDX_HEADS, dec, LANES), lambda bb, s, pt: (0, bb, 0)),
                  pl.BlockSpec((dec, LANES), lambda bb, s, pt: (bb, 0))]
        + _page_specs(npg, IDX_DIM, layer, n_pages, transposed=True)
        + [pl.BlockSpec((None, PAGE, LANES), lambda bb, s, pt: (bb, 0, 0))],
        out_specs=(pl.BlockSpec((None, dec, npg * PAGE), lambda bb, s, pt: (bb, 0, s)),
                   pl.BlockSpec((None, dec, LANES), lambda bb, s, pt: (bb, 0, 0))),
        scratch_shapes=[pltpu.VMEM((IDX_DIM, npg * PAGE), MXU_DTYPE)])
    return pl.pallas_call(
        functools.partial(_dsa_scores_kernel, npg),
        out_shape=(jax.ShapeDtypeStruct((b, dec, n_pages * PAGE), F32),
                   jax.ShapeDtypeStruct((b, dec, LANES), F32)),
        grid_spec=grid_spec,
        compiler_params=_cparams(2),
        name="dsa_scores",
    )(pt, aiq, misc, *([pool] * npg), newk)


def _dsa_sample_kernel(npg, n_keep, pt_ref, *refs):
    aq_ref, scfull_ref, scn_ref, scstep_ref = refs[:4]
    pages = refs[4:4 + npg]
    newkv_ref, o_ref, thr_ref, cut_ref, m_ref, l_ref, acc_ref, kvbuf = refs[4 + npg:12 + npg]
    dec = scn_ref.shape[0]
    past_len = scfull_ref.shape[1]
    s_id = pl.program_id(1)
    q = _stack_heads(aq_ref)

    def picked(key, pos):
        thr = thr_ref[:, 0:1]
        return (key > thr) | ((key == thr) & (pos < cut_ref[:, 0:1]))

    @pl.when(s_id == 0)
    def _():
        new_ok = _new_mask((dec, LANES), dec)
        key_new = jnp.where(new_ok, _order_key(scn_ref[...]), INT_MIN)
        key_all = _order_key(scfull_ref[...])
        pos_all = _lane(key_all.shape)
        pos_new = past_len + _lane(key_new.shape)

        def count(pred_all, pred_new):
            return (_lane_tree_sum(jnp.where(pred_all, 1, 0))
                    + jnp.sum(jnp.where(pred_new, 1, 0), axis=-1, keepdims=True))

        thr = jnp.maximum(_kth_largest(lambda c: count(key_all >= c, key_new >= c), n_keep, (dec, 1)),
                          INT_MIN + 1)
        thr_ref[...] = jnp.broadcast_to(thr, thr_ref.shape)
        nbits = int(past_len + LANES).bit_length()
        cut_ref[...] = jnp.full(cut_ref.shape, (1 << nbits) - 1, I32)

        @pl.when(jnp.max(count(key_all >= thr, key_new >= thr)) > n_keep)
        def _():
            need = n_keep - count(key_all > thr, key_new > thr)
            cut = _tie_cut(lambda c: count((key_all == thr) & (pos_all < c), (key_new == thr) & (pos_new < c)),
                           need, nbits, (dec, 1))
            cut_ref[...] = jnp.broadcast_to(cut, cut_ref.shape)

        m_ref[...] = jnp.full_like(m_ref, NEG)
        l_ref[...] = jnp.zeros_like(l_ref)
        acc_ref[...] = jnp.zeros_like(acc_ref)
        kv = newkv_ref[...]
        mask = _tile_rows(picked(key_new, pos_new), A_HEADS)
        _softmax_update(jnp.where(mask, _dot_nt(q, kv), NEG), mask, m_ref, l_ref, acc_ref, kv)

    for g in range(npg):
        kvbuf[g * PAGE:(g + 1) * PAGE, :] = _mx(pages[g][...])
    kv = kvbuf[...]
    key_step = _order_key(scstep_ref[...])
    mask = _tile_rows(picked(key_step, s_id * (npg * PAGE) + _lane(key_step.shape)), A_HEADS)
    _softmax_update(jnp.where(mask, _dot_nt(q, kv), NEG), mask, m_ref, l_ref, acc_ref, kv)

    @pl.when(s_id == pl.num_programs(1) - 1)
    def _():
        o = acc_ref[...] / l_ref[...]
        for hh in range(A_HEADS):
            o_ref[:, hh * HD:(hh + 1) * HD] = o[hh * dec:(hh + 1) * dec, HD:]


def _dsa_sample(aq, scp, scn, pool, layer, pt, newkv, b, dec, n_pages):
    npg = min(PAGES_PER_STEP, n_pages)
    n_keep = min(DSA_TOPK, (n_pages * PAGE + dec) // 4)
    grid_spec = pltpu.PrefetchScalarGridSpec(
        num_scalar_prefetch=1, grid=(b, n_pages // npg),
        in_specs=[pl.BlockSpec((A_HEADS, dec, LANES), lambda bb, s, pt: (0, bb, 0)),
                  pl.BlockSpec((None, dec, n_pages * PAGE), lambda bb, s, pt: (bb, 0, 0)),
                  pl.BlockSpec((None, dec, LANES), lambda bb, s, pt: (bb, 0, 0)),
                  pl.BlockSpec((None, dec, npg * PAGE), lambda bb, s, pt: (bb, 0, s))]
        + _page_specs(npg, 2 * HD, layer, n_pages)
        + [pl.BlockSpec((None, PAGE, LANES), lambda bb, s, pt: (bb, 0, 0))],
        out_specs=pl.BlockSpec((dec, A_HEADS * HD), lambda bb, s, pt: (bb, 0)),
        scratch_shapes=[pltpu.VMEM((dec, LANES), I32), pltpu.VMEM((dec, LANES), I32),
                        pltpu.VMEM((A_HEADS * dec, 1), F32), pltpu.VMEM((A_HEADS * dec, 1), F32),
                        pltpu.VMEM((A_HEADS * dec, LANES), F32),
                        pltpu.VMEM((npg * PAGE, LANES), MXU_DTYPE)])
    return pl.pallas_call(
        functools.partial(_dsa_sample_kernel, npg, n_keep),
        out_shape=jax.ShapeDtypeStruct((b * dec, A_HEADS * HD), F32),
        grid_spec=grid_spec,
        compiler_params=_cparams(2),
        name="dsa_sample",
    )(pt, aq, scp, scn, scp, *([pool] * npg), newkv)


def _mla_sample_kernel(npg, pt_ref, *refs):
    q_ref = refs[0]
    pages = refs[1:1 + npg]
    newkb_ref, wuv_ref, o_ref, m_ref, l_ref, acc_ref, kbuf = refs[1 + npg:8 + npg]
    dec = o_ref.shape[0]
    s_id = pl.program_id(1)
    q = _stack_heads(q_ref)
    wk = KV_LORA + D_ROPE

    @pl.when(s_id == 0)
    def _():
        kbuf[...] = jnp.zeros_like(kbuf)
        m_ref[...] = jnp.full_like(m_ref, NEG)
        l_ref[...] = jnp.zeros_like(l_ref)
        acc_ref[...] = jnp.zeros_like(acc_ref)
        k = newkb_ref[...]
        mask = _new_mask((B_HEADS * dec, PAGE), dec)
        _softmax_update(jnp.where(mask, _dot_nt(q, k), NEG), mask, m_ref, l_ref, acc_ref, k[:, :LANES])

    for g in range(npg):
        kbuf[0:wk, g * PAGE:(g + 1) * PAGE] = _mx(pages[g][...])
    _softmax_update(_dot(q, kbuf[...]), None, m_ref, l_ref, acc_ref, kbuf[0:KV_LORA, :], v_transposed=True)

    @pl.when(s_id == pl.num_programs(1) - 1)
    def _():
        o = _mx(acc_ref[...] / l_ref[...])
        for p in range(B_HEADS // 2):
            pair = jnp.concatenate([o[2 * p * dec:(2 * p + 1) * dec], o[(2 * p + 1) * dec:(2 * p + 2) * dec]], axis=1)
            o_ref[:, p * LANES:(p + 1) * LANES] = _dot(pair, wuv_ref[p])


def _mla_sample(qcat, pool, layer, pt, newkb, wuv, b, dec, n_pages):
    npg = min(PAGES_PER_STEP, n_pages)
    grid_spec = pltpu.PrefetchScalarGridSpec(
        num_scalar_prefetch=1, grid=(b, n_pages // npg),
        in_specs=[pl.BlockSpec((B_HEADS, dec, 2 * LANES), lambda bb, s, pt: (0, bb, 0))]
        + _page_specs(npg, KV_LORA + D_ROPE, layer, n_pages, transposed=True)
        + [pl.BlockSpec((None, PAGE, 2 * LANES), lambda bb, s, pt: (bb, 0, 0)),
           pl.BlockSpec(wuv.shape, lambda bb, s, pt: (0, 0, 0))],
        out_specs=pl.BlockSpec((dec, B_HEADS * D_V), lambda bb, s, pt: (bb, 0)),
        scratch_shapes=[pltpu.VMEM((B_HEADS * dec, 1), F32), pltpu.VMEM((B_HEADS * dec, 1), F32),
                        pltpu.VMEM((B_HEADS * dec, LANES), F32),
                        pltpu.VMEM((2 * LANES, npg * PAGE), MXU_DTYPE)])
    return pl.pallas_call(
        functools.partial(_mla_sample_kernel, npg),
        out_shape=jax.ShapeDtypeStruct((b * dec, B_HEADS * D_V), F32),
        grid_spec=grid_spec,
        compiler_params=_cparams(2),
        name="mla_sample",
    )(pt, qcat, *([pool] * npg), newkb, wuv)


def _nsa_select_kernel(n_cmp, n_slc, n_top, past_len, bps, cq_ref, cqr_ref, kvc_ref, band_ref,
                       state_ref, neww_ref, newwb_ref, ocw_ref, sel_ref, wstate_ref):
    dec = neww_ref.shape[0]
    q = _stack_heads(cq_ref)
    qr = _stack_heads(cqr_ref)
    rows_q = C_HEADS * dec

    kvc = kvc_ref[...]
    ncp = kvc.shape[0]
    qpos_c = past_len + _row((dec, ncp)) % dec
    n_id = _lane((dec, ncp))
    cmask = (n_id * CMP_STRIDE + CMP_LEN - 1 <= qpos_c) & (n_id < n_cmp)
    p_cmp = _masked_softmax(_dot_nt(q, kvc), _tile_rows(cmask, C_HEADS))
    ocw_ref[0] = _dot(_mx(p_cmp), kvc)
    pc = p_cmp[0:dec]
    for hh in range(1, C_HEADS):
        pc = pc + p_cmp[hh * dec:(hh + 1) * dec]

    imp = _importance(pc, band_ref[...])
    sel = _select_blocks(imp, _lane(imp.shape), past_len + _row(imp.shape), n_slc, n_top)
    for s in range(sel_ref.shape[0]):
        tile = sel[:, (s * bps // LANES) * LANES:(s * bps // LANES + 1) * LANES]
        shift = (s * bps) % LANES
        sel_ref[s] = pltpu.roll(tile, LANES - shift, 1) if shift else tile

    state = state_ref[...]
    nst = state.shape[0]
    sb = _mx(state)
    nb = newwb_ref[...]
    trow = _row((rows_q, nst)) % dec
    dist = nst + trow - _lane((rows_q, nst))
    m1 = (dist >= 0) & (dist < WINDOW)
    m2 = _new_mask((rows_q, PAGE), dec)
    s1 = jnp.where(m1, _dot_nt(qr, sb), NEG)
    s2 = jnp.where(m2, _dot_nt(qr, nb), NEG)
    m = jnp.maximum(jnp.max(s1, axis=-1, keepdims=True), jnp.max(s2, axis=-1, keepdims=True))
    e1 = jnp.where(m1, jnp.exp(s1 - m), 0.0)
    e2 = jnp.where(m2, jnp.exp(s2 - m), 0.0)
    den = jnp.maximum(jnp.sum(e1, axis=-1, keepdims=True) + jnp.sum(e2, axis=-1, keepdims=True),
                      jnp.finfo(F32).tiny)
    ocw_ref[1] = _dot(_mx(e1 / den), sb) + _dot(_mx(e2 / den), nb)

    wstate_ref[0:nst - dec, :] = state[dec:, :]
    wstate_ref[nst - dec:nst, :] = neww_ref[...]


def _nsa_select(cq, cqr, kvc, state, layer, neww, newwb, b, dec, n_pages, npg):
    past_len = n_pages * PAGE
    total = past_len + dec
    n_chunk = kvc.shape[1]
    n_cmp = total // CMP_STRIDE - CMP_LEN // CMP_STRIDE + 1
    n_slc = -(-total // SLC_BLOCK)
    n_top = min(N_SLC, n_slc)
    bps = npg * PAGE // SLC_BLOCK
    nsteps = n_pages // npg
    nsl = -(-max(n_slc, nsteps * bps) // LANES) * LANES
    band = _band_matrix(n_chunk, nsl)
    nst = state.shape[2]
    return pl.pallas_call(
        functools.partial(_nsa_select_kernel, n_cmp, n_slc, n_top, past_len, bps),
        out_shape=(jax.ShapeDtypeStruct((b, 2, C_HEADS * dec, LANES), F32),
                   jax.ShapeDtypeStruct((b, nsteps, dec, LANES), F32),
                   jax.ShapeDtypeStruct((b, nst, LANES), F32)),
        grid=(b,),
        in_specs=[pl.BlockSpec((C_HEADS, dec, LANES), lambda bb: (0, bb, 0)),
                  pl.BlockSpec((C_HEADS, dec, LANES), lambda bb: (0, bb, 0)),
                  pl.BlockSpec((None, n_chunk, LANES), lambda bb: (bb, 0, 0)),
                  pl.BlockSpec(band.shape, lambda bb: (0, 0)),
                  pl.BlockSpec((None, None, nst, LANES), lambda bb: (layer, bb, 0, 0)),
                  pl.BlockSpec((dec, LANES), lambda bb: (bb, 0)),
                  pl.BlockSpec((None, PAGE, LANES), lambda bb: (bb, 0, 0))],
        out_specs=(pl.BlockSpec((None, 2, C_HEADS * dec, LANES), lambda bb: (bb, 0, 0, 0)),
                   pl.BlockSpec((None, nsteps, dec, LANES), lambda bb: (bb, 0, 0, 0)),
                   pl.BlockSpec((None, nst, LANES), lambda bb: (bb, 0, 0))),
        compiler_params=_cparams(1),
        name="nsa_select",
    )(cq, cqr, kvc, band, state, neww, newwb)


def _nsa_slc_kernel(npg, pt_ref, *refs):
    cqr_ref, misc_ref, ocw_ref, sel_ref = refs[:4]
    pages = refs[4:4 + npg]
    newb_ref, o_ref, m_ref, l_ref, acc_ref, kvbuf = refs[4 + npg:10 + npg]
    dec = misc_ref.shape[0]
    s_id = pl.program_id(1)
    qr = _stack_heads(cqr_ref)

    @pl.when(s_id == 0)
    def _():
        m_ref[...] = jnp.full_like(m_ref, NEG)
        l_ref[...] = jnp.zeros_like(l_ref)
        acc_ref[...] = jnp.zeros_like(acc_ref)
        kv = newb_ref[...]
        mask = _new_mask((C_HEADS * dec, PAGE), dec)
        _softmax_update(jnp.where(mask, _dot_nt(qr, kv), NEG), mask, m_ref, l_ref, acc_ref, kv)

    sel = sel_ref[...]
    low = _lane((dec, PAGE)) < SLC_BLOCK
    masks = []
    for g in range(npg):
        kvbuf[g * PAGE:(g + 1) * PAGE, :] = _mx(pages[g][...])
        masks.append(jnp.where(low, sel[:, 2 * g:2 * g + 1], sel[:, 2 * g + 1:2 * g + 2]) > 0.5)
    mask = _tile_rows(jnp.concatenate(masks, axis=1), C_HEADS)
    kv = kvbuf[...]
    _softmax_update(jnp.where(mask, _dot_nt(qr, kv), NEG), mask, m_ref, l_ref, acc_ref, kv)

    @pl.when(s_id == pl.num_programs(1) - 1)
    def _():
        o_slc = acc_ref[...] / jnp.maximum(l_ref[...], jnp.finfo(F32).tiny)
        misc = misc_ref[...]
        o_cmp = ocw_ref[0]
        o_win = ocw_ref[1]
        for hh in range(C_HEADS):
            g = [misc[:, MISC_G + N_GATES * hh + k:MISC_G + N_GATES * hh + k + 1] for k in range(N_GATES)]
            sl = slice(hh * dec, (hh + 1) * dec)
            o = g[0] * o_cmp[sl] + g[1] * o_slc[sl] + g[2] * o_win[sl]
            o_ref[:, hh * HD:(hh + 1) * HD] = o[:, HD:]


def _nsa_slc(cqr, misc, ocw, selsteps, pool, layer, pt, newb, b, dec, n_pages, npg):
    grid_spec = pltpu.PrefetchScalarGridSpec(
        num_scalar_prefetch=1, grid=(b, n_pages // npg),
        in_specs=[pl.BlockSpec((C_HEADS, dec, LANES), lambda bb, s, pt: (0, bb, 0)),
                  pl.BlockSpec((dec, LANES), lambda bb, s, pt: (bb, 0)),
                  pl.BlockSpec((None, 2, C_HEADS * dec, LANES), lambda bb, s, pt: (bb, 0, 0, 0)),
                  pl.BlockSpec((None, None, dec, LANES), lambda bb, s, pt: (bb, s, 0, 0))]
        + _page_specs(npg, 2 * HD, layer, n_pages)
        + [pl.BlockSpec((None, PAGE, LANES), lambda bb, s, pt: (bb, 0, 0))],
        out_specs=pl.BlockSpec((dec, C_HEADS * HD), lambda bb, s, pt: (bb, 0)),
        scratch_shapes=[pltpu.VMEM((C_HEADS * dec, 1), F32), pltpu.VMEM((C_HEADS * dec, 1), F32),
                        pltpu.VMEM((C_HEADS * dec, LANES), F32),
                        pltpu.VMEM((npg * PAGE, LANES), MXU_DTYPE)])
    return pl.pallas_call(
        functools.partial(_nsa_slc_kernel, npg),
        out_shape=jax.ShapeDtypeStruct((b * dec, C_HEADS * HD), F32),
        grid_spec=grid_spec,
        compiler_params=_cparams(2),
        name="nsa_slc",
    )(pt, cqr, misc, ocw, selsteps, *([pool] * npg), newb)


def _new_pages(rows, b, dec):
    r = rows.reshape(b, dec, rows.shape[-1])
    return _mx(jnp.pad(r, ((0, 0), (0, PAGE - dec), (0, 0))))


def _layer_weights(l, p):
    wuq, wuk, wuv = _layout_mla(p['b_w_uq'][l], p['b_w_uk'][l], p['b_w_uv'][l])
    wout = _mx(p['w_out'][l])
    na, nb = A_HEADS * HD, B_HEADS * D_V
    zero = jnp.zeros((HD, HD), F32)
    return dict(
        w_in=_layout_w_in(p['w_in'][l]), wuq=wuq, wuk=wuk, wuv=wuv,
        wout=(wout[:na], wout[na:na + nb], wout[na + nb:]),
        wp=jnp.concatenate([p['c_cmp_pos_k'][l], p['c_cmp_pos_v'][l]], axis=1),
        wproj=_mx(jnp.block([[p['c_cmp_proj_k'][l], zero], [zero, p['c_cmp_proj_v'][l]]])),
        pre13=_mx(p['ffn_pre_w13'][l]), pre2=_mx(p['ffn_pre_w2'][l]),
        post13=_mx(p['ffn_post_w13'][l]), post2=_mx(p['ffn_post_w2'][l]))


def _prompt_mixers(proj, lw, b, t):
    (aq, akv, akvb, aiq, misc, aikb, qcat, bckv, kb, cq, cqr, ccmp, cslc, cslcb, cwin, cwinb) = proj
    o_a = _dsa_prompt(aiq, misc, aq, aikb, akvb, b, t)
    o_b = _mla_prompt(qcat, kb, lw['wuv'], b, t)
    n_chunk = t // CMP_STRIDE
    n_cmp = n_chunk - CMP_LEN // CMP_STRIDE + 1
    psum = _cmp_rows(ccmp, lw['wp'], b, t).reshape(b, n_chunk, 2 * LANES)
    kvc = _cmp_finish(psum, lw['wproj'], n_cmp)
    o_c = _nsa_prompt(cq, cqr, misc, kvc, cslcb, cwinb, b, t)
    nw = min(WINDOW, t)
    rows = (akv.reshape(b, t, -1), misc.reshape(b, t, -1)[..., :IDX_DIM],
            bckv.reshape(b, t, -1)[..., :KV_LORA + D_ROPE], ccmp.reshape(b, t, -1),
            cslc.reshape(b, t, -1), cwin.reshape(b, t, -1)[:, t - nw:])
    return (o_a, o_b, o_c), rows


def _sample_mixers(proj, lw, l, b, dec, pools, state_win, pt, n_pages):
    (aq, akv, akvb, aiq, misc, aikb, qcat, bckv, kb, cq, cqr, ccmp, cslc, cslcb, cwin, cwinb) = proj
    npg = min(PAGES_PER_STEP, n_pages)
    scp, scn = _dsa_scores(aiq, misc, jnp.swapaxes(pools['a_kidx'], 2, 3), l, pt, _new_pages(aikb, b, dec),
                           b, dec, n_pages)
    o_a = _dsa_sample(aq, scp, scn, pools['a_kv'], l, pt, _new_pages(akvb, b, dec), b, dec, n_pages)
    o_b = _mla_sample(qcat, jnp.swapaxes(pools['b_ckv'], 2, 3), l, pt, _new_pages(kb, b, dec), lw['wuv'],
                      b, dec, n_pages)
    total = n_pages * PAGE + dec
    n_cmp = total // CMP_STRIDE - CMP_LEN // CMP_STRIDE + 1
    psum = _cmp_pages(pools['c_cmp'], l, pt, lw['wp'], b, n_pages)
    kvc = _cmp_finish(psum, lw['wproj'], n_cmp)
    ocw, selsteps, wstate = _nsa_select(cq, cqr, kvc, state_win, l, cwin, _new_pages(cwinb, b, dec),
                                        b, dec, n_pages, npg)
    o_c = _nsa_slc(cqr, misc, ocw, selsteps, pools['c_slc'], l, pt, _new_pages(cslcb, b, dec),
                   b, dec, n_pages, npg)
    rows = (akv.reshape(b, dec, -1), misc.reshape(b, dec, -1)[..., :IDX_DIM],
            bckv.reshape(b, dec, -1)[..., :KV_LORA + D_ROPE], ccmp.reshape(b, dec, -1),
            cslc.reshape(b, dec, -1), wstate)
    return (o_a, o_b, o_c), rows


def _trunk(x, tab, qdt, mixers, weights, p):
    b, t, d = x.shape
    x = x.reshape(b * t, d)
    depth = len(weights)
    per_layer = []
    for l, lw in enumerate(weights):
        x = _ffn(x, p['norm_ffn_pre'][l], lw['pre13'], lw['pre2'])
        proj = _inproj(x, p['norm_mix'][l], lw['w_in'], tab, p['b_q_norm'][l], lw['wuq'], lw['wuk'],
                       p['b_kv_norm'][l], qdt)
        o, rows = mixers(proj, lw, l)
        x = _ffn(x, p['norm_ffn_post'][l], lw['post13'], lw['post2'], mix=tuple(zip(o, lw['wout'])),
                 final_g=p['norm_final'] if l == depth - 1 else None)
        per_layer.append(rows)
    stacked = tuple(jnp.stack(r, axis=0) for r in zip(*per_layer))
    return x.reshape(b, t, d), stacked


def kernel(x_prompt, x_sample, cache_a_kv, cache_a_kidx, cache_b_ckv, cache_c_cmp, cache_c_slc,
           state_c_win, page_table, w_in, w_out, b_w_uq, b_w_uk, b_w_uv, b_q_norm, b_kv_norm,
           c_cmp_pos_k, c_cmp_pos_v, c_cmp_proj_k, c_cmp_proj_v,
           ffn_pre_w13, ffn_pre_w2, ffn_post_w13, ffn_post_w2,
           norm_ffn_pre, norm_mix, norm_ffn_post, norm_final):
    p = {'w_in': w_in, 'w_out': w_out, 'b_w_uq': b_w_uq, 'b_w_uk': b_w_uk, 'b_w_uv': b_w_uv,
         'b_q_norm': b_q_norm, 'b_kv_norm': b_kv_norm,
         'c_cmp_pos_k': c_cmp_pos_k, 'c_cmp_pos_v': c_cmp_pos_v,
         'c_cmp_proj_k': c_cmp_proj_k, 'c_cmp_proj_v': c_cmp_proj_v,
         'ffn_pre_w13': ffn_pre_w13, 'ffn_pre_w2': ffn_pre_w2,
         'ffn_post_w13': ffn_post_w13, 'ffn_post_w2': ffn_post_w2,
         'norm_ffn_pre': norm_ffn_pre, 'norm_mix': norm_mix, 'norm_ffn_post': norm_ffn_post,
         'norm_final': norm_final}
    pools = {'a_kv': cache_a_kv, 'a_kidx': cache_a_kidx, 'b_ckv': cache_b_ckv,
             'c_cmp': cache_c_cmp, 'c_slc': cache_c_slc}
    depth = w_in.shape[0]
    weights = [_layer_weights(l, p) for l in range(depth)]
    bp, tp, _ = x_prompt.shape
    bs, dec, _ = x_sample.shape
    n_pages = page_table.shape[1]
    past_len = n_pages * PAGE
    pt = page_table.reshape(-1).astype(I32)

    tab_p = _rope_table(jnp.arange(tp, dtype=I32))
    tab_s = _rope_table(jnp.tile(past_len + jnp.arange(dec, dtype=I32), bs))

    y_p, rows_p = _trunk(x_prompt, tab_p, MXU_DTYPE,
                         lambda proj, lw, l: _prompt_mixers(proj, lw, bp, tp), weights, p)
    y_s, rows_s = _trunk(x_sample, tab_s, F32,
                         lambda proj, lw, l: _sample_mixers(proj, lw, l, bs, dec, pools, state_c_win, pt, n_pages),
                         weights, p)
    out = [y_p, y_s]
    for rp, rs in zip(rows_p, rows_s):
        out += [rp, rs]
    return tuple(out)
```
